```python
import jax, jax.numpy as jnp
from jax import lax
import numpy as np

D_MODEL = 2048
BATCH = 4
SEQ = 8192
DEPTH = 1

CHUNK = 64
LEFT_CHUNKS = 8
BAND = (LEFT_CHUNKS + 1) * CHUNK
ATT_HEADS = 16
ATT_HEAD_DIM = 64
ATT_WIDTH = ATT_HEADS * ATT_HEAD_DIM
REL_CLIP = 256
REL_FUTURE = CHUNK - 1
N_REL = REL_CLIP + REL_FUTURE + 1
SSD_HEADS = 16
SSD_HEAD_DIM = 64
SSD_WIDTH = SSD_HEADS * SSD_HEAD_DIM
SSD_GROUPS = 2
SSD_STATE = 128
SSD_CONV = 4
SSD_CHUNK = CHUNK
XBC_WIDTH = SSD_WIDTH + 2 * SSD_GROUPS * SSD_STATE
MIX_WIDTH = ATT_WIDTH + SSD_WIDTH
IN_COLS = 3 * ATT_WIDTH + SSD_WIDTH + XBC_WIDTH + SSD_HEADS
FFN_HIDDEN = -(-8 * D_MODEL // (3 * 256)) * 256
N_MOD = 6
EPS = 1e-6

kernel_name = "hymba_chunk_attn_ssd_adaln"


def rmsnorm(x, g):
    xf = x.astype(jnp.float32)
    y = xf * lax.rsqrt(jnp.mean(xf * xf, axis=-1, keepdims=True) + EPS)
    return (y * g.astype(jnp.float32)).astype(x.dtype)


def chunk_attention(q, k, v, rel_bias):
    b, s, h, dh = q.shape
    n_chunks = s // CHUNK
    pad = LEFT_CHUNKS * CHUNK
    k_pad = jnp.pad(k, ((0, 0), (pad, 0), (0, 0), (0, 0)))
    v_pad = jnp.pad(v, ((0, 0), (pad, 0), (0, 0), (0, 0)))
    q_loc = jnp.arange(CHUNK)[:, None] + pad
    k_loc = jnp.arange(BAND)[None, :]
    rel_idx = jnp.clip(q_loc - k_loc, -REL_FUTURE, REL_CLIP) + REL_FUTURE
    bias = rel_bias[:, rel_idx].astype(jnp.float32)
    scale = ATT_HEAD_DIM ** -0.5

    def one_chunk(i):
        start = i * CHUNK
        qc = lax.dynamic_slice_in_dim(q, start, CHUNK, axis=1)
        kc = lax.dynamic_slice_in_dim(k_pad, start, BAND, axis=1)
        vc = lax.dynamic_slice_in_dim(v_pad, start, BAND, axis=1)
        scores = jnp.einsum("bqhd,bkhd->bhqk", qc, kc).astype(jnp.float32) * scale + bias
        valid = (start - pad + jnp.arange(BAND)) >= 0
        scores = jnp.where(valid, scores, -jnp.inf)
        probs = jax.nn.softmax(scores, axis=-1).astype(vc.dtype)
        return jnp.einsum("bhqk,bkhd->bqhd", probs, vc)

    out = lax.map(one_chunk, jnp.arange(n_chunks))
    return jnp.moveaxis(out, 0, 1).reshape(b, s, h * dh)


def ssd_mixer(xbc_raw, z, dt_raw, conv_w, conv_b, dt_bias, a_log, d_skip, norm_g):
    b, s, ch = xbc_raw.shape
    xbc = lax.conv_general_dilated(
        xbc_raw, conv_w[:, None, :].astype(xbc_raw.dtype), window_strides=(1,),
        padding=[(SSD_CONV - 1, 0)], dimension_numbers=("NWC", "WIO", "NWC"),
        feature_group_count=ch) + conv_b
    xbc = jax.nn.silu(xbc)
    xs, bm, cm = jnp.split(xbc, [SSD_WIDTH, SSD_WIDTH + SSD_GROUPS * SSD_STATE], axis=-1)
    nc = s // SSD_CHUNK
    r = SSD_HEADS // SSD_GROUPS
    dt = jax.nn.softplus((dt_raw + dt_bias).astype(jnp.float32))
    a = -jnp.exp(a_log.astype(jnp.float32))
    x = xs.reshape(b, nc, SSD_CHUNK, SSD_GROUPS, r, SSD_HEAD_DIM)
    bm = bm.reshape(b, nc, SSD_CHUNK, SSD_GROUPS, SSD_STATE)
    cm = cm.reshape(b, nc, SSD_CHUNK, SSD_GROUPS, SSD_STATE)
    dt = dt.reshape(b, nc, SSD_CHUNK, SSD_GROUPS, r)
    xdt = x * dt[..., None].astype(x.dtype)
    a_dt = jnp.moveaxis(dt * a.reshape(SSD_GROUPS, r), 2, -1)
    cs = jnp.cumsum(a_dt, axis=-1)
    causal = jnp.tril(jnp.ones((SSD_CHUNK, SSD_CHUNK), dtype=bool))
    seg = jnp.exp(jnp.where(causal, cs[..., :, None] - cs[..., None, :], -jnp.inf))
    cb = jnp.einsum("bclgn,bcsgn->bcgls", cm, bm)
    y_diag = jnp.einsum("bcgls,bcgrls,bcsgrp->bclgrp", cb, seg, xdt)
    decay = jnp.exp(cs[..., -1:] - cs)
    states = jnp.einsum("bclgn,bcgrl,bclgrp->bcgrpn", bm, decay, xdt).astype(jnp.float32)
    chunk_decay = jnp.exp(cs[..., -1])

    def step(h, inp):
        st, dec = inp
        return dec[..., None, None] * h + st, h

    h0 = jnp.zeros((b, SSD_GROUPS, r, SSD_HEAD_DIM, SSD_STATE), jnp.float32)
    _, prev = lax.scan(step, h0, (jnp.moveaxis(states, 1, 0), jnp.moveaxis(chunk_decay, 1, 0)))
    prev = jnp.moveaxis(prev, 0, 1)
    y_off = jnp.einsum("bclgn,bcgrpn,bcgrl->bclgrp", cm, prev, jnp.exp(cs))
    y = y_diag + y_off + x * d_skip.reshape(SSD_GROUPS, r)[:, :, None]
    y = y.reshape(b, s, SSD_WIDTH).astype(xs.dtype)
    return rmsnorm(y * jax.nn.silu(z), norm_g)


def setup_inputs(seed: int = 0) -> dict:
    key = jax.random.key(seed)
    ks = jax.random.split(key, 24)
    f32 = jnp.float32
    nrm = lambda k, shape, s: jax.random.normal(k, shape, f32) * s
    gain = lambda k, shape: 1.0 + 0.01 * jax.random.normal(k, shape, f32)
    dt0 = jnp.exp(jax.random.uniform(ks[10], (DEPTH, SSD_HEADS), f32,
                                     jnp.log(1e-3), jnp.log(1e-1)))
    return {
        "x": nrm(ks[0], (BATCH, SEQ, D_MODEL), 1.0),
        "c": nrm(ks[1], (BATCH, D_MODEL), 1.0),
        "w_ada": nrm(ks[2], (DEPTH, D_MODEL, N_MOD * D_MODEL), D_MODEL ** -0.5),
        "b_ada": nrm(ks[3], (DEPTH, N_MOD * D_MODEL), 0.01),
        "g_mix": gain(ks[4], (DEPTH, D_MODEL)),
        "w_in": nrm(ks[5], (DEPTH, D_MODEL, IN_COLS), D_MODEL ** -0.5),
        "rel_bias": nrm(ks[6], (DEPTH, ATT_HEADS, N_REL), 0.5),
        "conv_w": nrm(ks[7], (DEPTH, SSD_CONV, XBC_WIDTH), SSD_CONV ** -0.5),
        "conv_b": nrm(ks[8], (DEPTH, XBC_WIDTH), 0.01),
        "dt_bias": dt0 + jnp.log(-jnp.expm1(-dt0)),
        "a_log": jnp.log(jax.random.uniform(ks[11], (DEPTH, SSD_HEADS), f32, 1.0, 16.0)),
        "d_skip": gain(ks[12], (DEPTH, SSD_HEADS)),
        "g_att_out": gain(ks[13], (DEPTH, ATT_WIDTH)),
        "g_ssd_out": gain(ks[14], (DEPTH, SSD_WIDTH)),
        "w_out": nrm(ks[15], (DEPTH, MIX_WIDTH, D_MODEL), MIX_WIDTH ** -0.5),
        "g_ffn": gain(ks[16], (DEPTH, D_MODEL)),
        "w_gate": nrm(ks[17], (DEPTH, D_MODEL, FFN_HIDDEN), D_MODEL ** -0.5),
        "w_up": nrm(ks[18], (DEPTH, D_MODEL, FFN_HIDDEN), D_MODEL ** -0.5),
        "w_down": nrm(ks[19], (DEPTH, FFN_HIDDEN, D_MODEL), FFN_HIDDEN ** -0.5),
        "g_final": gain(ks[20], (D_MODEL,)),
    }


def reference(x, c, w_ada, b_ada, g_mix, w_in, rel_bias, conv_w, conv_b, dt_bias, a_log,
              d_skip, g_att_out, g_ssd_out, w_out, g_ffn, w_gate, w_up, w_down, g_final):
    b, s, _ = x.shape
    cond = jax.nn.silu(c)
    splits = [ATT_WIDTH, 2 * ATT_WIDTH, 3 * ATT_WIDTH, 3 * ATT_WIDTH + SSD_WIDTH,
              3 * ATT_WIDTH + SSD_WIDTH + XBC_WIDTH]
    for l in range(DEPTH):
        mods = cond @ w_ada[l] + b_ada[l]
        sh1, sc1, gt1, sh2, sc2, gt2 = [m[:, None, :] for m in jnp.split(mods, N_MOD, axis=-1)]
        h = rmsnorm(x, g_mix[l]) * (1.0 + sc1) + sh1
        proj = h @ w_in[l]
        q, k, v, z, xbc, dt_raw = jnp.split(proj, splits, axis=-1)
        att = chunk_attention(q.reshape(b, s, ATT_HEADS, ATT_HEAD_DIM),
                              k.reshape(b, s, ATT_HEADS, ATT_HEAD_DIM),
                              v.reshape(b, s, ATT_HEADS, ATT_HEAD_DIM), rel_bias[l])
        att = rmsnorm(att, g_att_out[l])
        ssd = ssd_mixer(xbc, z, dt_raw, conv_w[l], conv_b[l], dt_bias[l], a_log[l],
                        d_skip[l], g_ssd_out[l])
        mix = jnp.concatenate([att, ssd], axis=-1) @ w_out[l]
        x = x + gt1 * mix
        h = rmsnorm(x, g_ffn[l]) * (1.0 + sc2) + sh2
        ffn = (jax.nn.silu(h @ w_gate[l]) * (h @ w_up[l])) @ w_down[l]
        x = x + gt2 * ffn
    return rmsnorm(x, g_final)
```

```python
import functools

import jax
import jax.numpy as jnp
from jax import lax
from jax.experimental import pallas as pl
from jax.experimental.pallas import tpu as pltpu

F32 = jnp.float32
BF16 = jnp.bfloat16

D_MODEL = 2048
CHUNK = 64
LEFT_CHUNKS = 8
BAND = (LEFT_CHUNKS + 1) * CHUNK
ATT_HEADS = 16
ATT_HEAD_DIM = 64
ATT_WIDTH = ATT_HEADS * ATT_HEAD_DIM
REL_CLIP = 256
REL_FUTURE = CHUNK - 1
SSD_HEADS = 16
SSD_HEAD_DIM = 64
SSD_WIDTH = SSD_HEADS * SSD_HEAD_DIM
SSD_GROUPS = 2
SSD_STATE = 128
SSD_CONV = 4
XBC_WIDTH = SSD_WIDTH + 2 * SSD_GROUPS * SSD_STATE
GROUP_WIDTH = SSD_WIDTH // SSD_GROUPS
FFN_HIDDEN = 5632
N_MOD = 6
EPS = 1e-6

LANES = 128
HEAD_PAIRS = ATT_WIDTH // LANES
VMEM_LIMIT = 56 * 1024 * 1024

ADA_TN = 1024
PROJ_TM = 512
PROJ_TN = 512
ATT_TQ = 512
SSD_TB = 512
SSD_L = 64
OUT_TM = 512
FFN_TM = 512
FFN_TH = 512

HIGHEST = lax.Precision.HIGHEST


def _params(*sem):
    return pltpu.CompilerParams(dimension_semantics=sem, vmem_limit_bytes=VMEM_LIMIT)


def _resident(shape):
    nd = len(shape)
    return pl.BlockSpec(shape, lambda *_: (0,) * nd, pipeline_mode=pl.Buffered(1))


def _rms(x):
    return x * lax.rsqrt(jnp.mean(x * x, axis=-1, keepdims=True) + EPS)


def _silu(x):
    return x * jax.nn.sigmoid(x)


def _adaln_kernel(c_ref, w_ref, b_ref, o_ref):
    cond = _silu(c_ref[...])
    o_ref[...] = jnp.dot(cond.astype(BF16), w_ref[...].astype(BF16),
                         preferred_element_type=F32) + b_ref[...]


def _adaln(c_pad, w, b):
    rows, d = c_pad.shape
    n = w.shape[1]
    return pl.pallas_call(
        _adaln_kernel,
        grid=(n // ADA_TN,),
        in_specs=[pl.BlockSpec((rows, d), lambda j: (0, 0)),
                  pl.BlockSpec((d, ADA_TN), lambda j: (0, j)),
                  pl.BlockSpec((1, ADA_TN), lambda j: (0, j))],
        out_specs=pl.BlockSpec((rows, ADA_TN), lambda j: (0, j)),
        out_shape=jax.ShapeDtypeStruct((rows, n), F32),
        compiler_params=_params("arbitrary"),
        name="adaln",
    )(c_pad, w, b)


_PROJ_SPLITS = (ATT_WIDTH, ATT_WIDTH, ATT_WIDTH, SSD_WIDTH, XBC_WIDTH)


def _inproj_kernel(x_ref, mods_ref, g_ref, w_ref, wdt_ref,
                   q_ref, k_ref, v_ref, z_ref, xbc_ref, dt_ref, h_s):
    x = x_ref[...]
    h = _rms(x) * g_ref[...]
    h = h * (1.0 + mods_ref[1:2, :]) + mods_ref[0:1, :]
    h_s[...] = h.astype(BF16)
    col = 0
    for ref, width in zip((q_ref, k_ref, v_ref, z_ref, xbc_ref), _PROJ_SPLITS):
        for j in range(0, width, PROJ_TN):
            ref[:, j:j + PROJ_TN] = jnp.dot(
                h_s[...], w_ref[:, col + j:col + j + PROJ_TN],
                preferred_element_type=F32).astype(ref.dtype)
        col += width
    dt_ref[...] = jnp.dot(h_s[...], wdt_ref[...], preferred_element_type=F32)


def _inproj(x, mods, g, w_main, w_dt):
    b, s, d = x.shape
    tm = PROJ_TM
    row = lambda bi, i: (bi, i, 0)
    out_shape = [jax.ShapeDtypeStruct((b, s, w), BF16) for w in _PROJ_SPLITS]
    out_shape.append(jax.ShapeDtypeStruct((b, s, LANES), F32))
    out_specs = [pl.BlockSpec((None, tm, w), row) for w in _PROJ_SPLITS]
    out_specs.append(pl.BlockSpec((None, tm, LANES), row))
    return pl.pallas_call(
        _inproj_kernel,
        grid=(b, s // tm),
        in_specs=[pl.BlockSpec((None, tm, d), row),
                  pl.BlockSpec((None, N_MOD, d), lambda bi, i: (bi, 0, 0)),
                  _resident(g.shape), _resident(w_main.shape), _resident(w_dt.shape)],
        out_specs=out_specs,
        out_shape=out_shape,
        scratch_shapes=[pltpu.VMEM((tm, d), BF16)],
        compiler_params=_params("arbitrary", "arbitrary"),
        name="inproj",
    )(x, mods, g, w_main, w_dt)


def _attn_kernel(q_ref, kp_ref, kc_ref, vp_ref, vc_ref, bias_ref, g_ref, o_ref, k_s, v_s):
    i = pl.program_id(1)
    tq = ATT_TQ
    k_s[0:tq, :] = kp_ref[...]
    k_s[tq:, :] = kc_ref[...]
    v_s[0:tq, :] = vp_ref[...]
    v_s[tq:, :] = vc_ref[...]

    band_row = lax.broadcasted_iota(jnp.int32, (BAND, LANES), 0)
    rr = lax.broadcasted_iota(jnp.int32, (2 * CHUNK, LANES), 0)
    ll = lax.broadcasted_iota(jnp.int32, (2 * CHUNK, LANES), 1)
    own_head = (rr < CHUNK) == (ll < ATT_HEAD_DIM)
    low_lanes = lax.broadcasted_iota(jnp.int32, (CHUNK, LANES), 1) < ATT_HEAD_DIM
    scale = ATT_HEAD_DIM ** -0.5

    def chunk_body(c, carry):
        q0 = pl.multiple_of(c * CHUNK, CHUNK)
        n_before = jnp.maximum(LEFT_CHUNKS - (i * (tq // CHUNK) + c), 0) * CHUNK
        valid = band_row >= n_before
        outs = []
        for p in range(HEAD_PAIRS):
            lanes = slice(p * LANES, (p + 1) * LANES)
            qp = q_ref[pl.ds(q0, CHUNK), lanes].astype(F32) * scale
            q_bd = jnp.where(own_head, jnp.concatenate([qp, qp], axis=0), 0.0).astype(BF16)
            kb = k_s[pl.ds(q0, BAND), lanes]
            st = lax.dot_general(kb, q_bd, (((1,), (1,)), ((), ())),
                                 preferred_element_type=F32)
            st = jnp.where(valid, st + bias_ref[p], -jnp.inf)
            m = jnp.max(st, axis=0, keepdims=True)
            e = jnp.exp(st - m)
            den = jnp.sum(e, axis=0, keepdims=True)
            pt = (e * (1.0 / den)).astype(BF16)
            vb = v_s[pl.ds(q0, BAND), lanes]
            o2 = lax.dot_general(pt, vb, (((0,), (0,)), ((), ())),
                                 preferred_element_type=F32)
            outs.append(jnp.where(low_lanes, o2[:CHUNK], o2[CHUNK:]))
        att = jnp.concatenate(outs, axis=-1)
        o_ref[pl.ds(q0, CHUNK), :] = (_rms(att) * g_ref[...]).astype(o_ref.dtype)
        return carry

    lax.fori_loop(0, tq // CHUNK, chunk_body, 0)


def _attention(q, k, v, bias_t, g):
    b, s, w = q.shape
    tq = ATT_TQ
    cur = lambda bi, i: (bi, i, 0)
    prev = lambda bi, i: (bi, jnp.maximum(i - 1, 0), 0)
    blk = (None, tq, w)
    return pl.pallas_call(
        _attn_kernel,
        grid=(b, s // tq),
        in_specs=[pl.BlockSpec(blk, cur),
                  pl.BlockSpec(blk, prev), pl.BlockSpec(blk, cur),
                  pl.BlockSpec(blk, prev), pl.BlockSpec(blk, cur),
                  _resident(bias_t.shape), _resident(g.shape)],
        out_specs=pl.BlockSpec(blk, cur),
        out_shape=jax.ShapeDtypeStruct((b, s, w), BF16),
        scratch_shapes=[pltpu.VMEM((2 * tq, w), BF16), pltpu.VMEM((2 * tq, w), BF16)],
        compiler_params=_params("arbitrary", "arbitrary"),
        name="chunk_attn",
    )(q, k, k, v, v, bias_t, g)


def _bias_table(rel_bias):
    q_loc = jnp.arange(CHUNK)[:, None] + LEFT_CHUNKS * CHUNK
    k_loc = jnp.arange(BAND)[None, :]
    rel_idx = jnp.clip(q_loc - k_loc, -REL_FUTURE, REL_CLIP) + REL_FUTURE
    bias = rel_bias[:, rel_idx].astype(F32)
    bias = bias.reshape(HEAD_PAIRS, 2, CHUNK, BAND)
    return jnp.transpose(bias, (0, 3, 1, 2)).reshape(HEAD_PAIRS, BAND, LANES)


_HALO = 8


def _ssd_kernel(xbc_ref, z_ref, dt_ref, convw_ref, convb_ref, dtb_ref, a_ref, dskip_ref,
                expand_ref, g_ref, o_ref, raw_s, act_s, dt_s, y_s, h_s):
    i = pl.program_id(1)
    tb = SSD_TB
    L = SSD_L

    @pl.when(i == 0)
    def _():
        raw_s[0:_HALO, :] = jnp.zeros((_HALO, XBC_WIDTH), F32)
        h_s[...] = jnp.zeros_like(h_s)

    @pl.when(i > 0)
    def _():
        raw_s[0:_HALO, :] = raw_s[tb:tb + _HALO, :]

    raw_s[_HALO:, :] = xbc_ref[...].astype(F32)
    conv = convb_ref[...]
    for j in range(SSD_CONV):
        off = _HALO - (SSD_CONV - 1) + j
        conv = conv + convw_ref[j:j + 1, :] * raw_s[off:off + tb, :]
    act_s[...] = _silu(conv)

    dtr = dt_ref[...] + dtb_ref[...]
    dt_s[...] = jnp.maximum(dtr, 0.0) + jnp.log1p(jnp.exp(-jnp.abs(dtr)))

    a = -jnp.exp(a_ref[...])
    ri = lax.broadcasted_iota(jnp.int32, (L, L), 0)
    ci = lax.broadcasted_iota(jnp.int32, (L, L), 1)
    causal = ri >= ci
    tril = causal.astype(F32)
    low_lanes = lax.broadcasted_iota(jnp.int32, (L, LANES), 1) < SSD_HEAD_DIM
    expand = expand_ref[...]
    dskip = dskip_ref[...]
    bc0 = SSD_WIDTH
    cc0 = SSD_WIDTH + SSD_GROUPS * SSD_STATE

    def chunk_body(c, carry):
        r0 = pl.multiple_of(c * L, L)
        dtc = dt_s[pl.ds(r0, L), :]
        adt = dtc * a
        cs = jnp.dot(tril, adt, precision=HIGHEST, preferred_element_type=F32)
        cs_t = cs.T
        last = cs[L - 1:L, :]
        stacked = jnp.concatenate(
            [dtc, jnp.exp(last - cs), jnp.exp(cs), jnp.broadcast_to(jnp.exp(last), (8, LANES))], axis=0)
        wide = jnp.dot(stacked, expand, precision=HIGHEST, preferred_element_type=F32)
        dt_w, decay_w, ecs_w, cdec_w = wide[0:L], wide[L:2 * L], wide[2 * L:3 * L], wide[3 * L:3 * L + 1]

        xs = act_s[pl.ds(r0, L), 0:SSD_WIDTH]
        xdt = xs * dt_w
        xdt_b = xdt.astype(BF16)
        xdec_b = (xdt * decay_w).astype(BF16)
        ys = []
        for g in range(SSD_GROUPS):
            gl = slice(g * GROUP_WIDTH, (g + 1) * GROUP_WIDTH)
            bm = act_s[pl.ds(r0, L), bc0 + g * SSD_STATE:bc0 + (g + 1) * SSD_STATE].astype(BF16)
            cm = act_s[pl.ds(r0, L), cc0 + g * SSD_STATE:cc0 + (g + 1) * SSD_STATE].astype(BF16)
            cb = lax.dot_general(cm, bm, (((1,), (1,)), ((), ())), preferred_element_type=F32)
            hprev = h_s[g]
            y_off = jnp.dot(cm, hprev.astype(BF16), preferred_element_type=F32) * ecs_w[:, gl]
            states = lax.dot_general(bm, xdec_b[:, gl], (((0,), (0,)), ((), ())),
                                     preferred_element_type=F32)
            h_s[g] = hprev * cdec_w[:, gl] + states
            yd = []
            for pr in range(GROUP_WIDTH // LANES):
                hp = g * (GROUP_WIDTH // LANES) + pr
                xp = xdt_b[:, hp * LANES:(hp + 1) * LANES]
                halves = []
                for hh in (2 * hp, 2 * hp + 1):
                    seg = jnp.exp(jnp.where(causal, cs[:, hh:hh + 1] - cs_t[hh:hh + 1, :], -jnp.inf))
                    halves.append(jnp.dot((cb * seg).astype(BF16), xp, preferred_element_type=F32))
                yd.append(jnp.where(low_lanes, halves[0], halves[1]))
            ys.append(jnp.concatenate(yd, axis=-1) + y_off)
        y = jnp.concatenate(ys, axis=-1) + xs * dskip
        y_s[pl.ds(r0, L), :] = y
        return carry

    lax.fori_loop(0, tb // L, chunk_body, 0)
    gated = y_s[...] * _silu(z_ref[...].astype(F32))
    o_ref[...] = (_rms(gated) * g_ref[...]).astype(o_ref.dtype)


def _ssd(xbc, z, dt_raw, conv_w, conv_b, dt_bias, a_log, dskip_w, expand, g):
    b, s, _ = xbc.shape
    tb = SSD_TB
    row = lambda bi, i: (bi, i, 0)
    return pl.pallas_call(
        _ssd_kernel,
        grid=(b, s // tb),
        in_specs=[pl.BlockSpec((None, tb, XBC_WIDTH), row),
                  pl.BlockSpec((None, tb, SSD_WIDTH), row),
                  pl.BlockSpec((None, tb, LANES), row),
                  _resident(conv_w.shape), _resident(conv_b.shape), _resident(dt_bias.shape),
                  _resident(a_log.shape), _resident(dskip_w.shape), _resident(expand.shape),
                  _resident(g.shape)],
        out_specs=pl.BlockSpec((None, tb, SSD_WIDTH), row),
        out_shape=jax.ShapeDtypeStruct((b, s, SSD_WIDTH), BF16),
        scratch_shapes=[pltpu.VMEM((_HALO + tb, XBC_WIDTH), F32),
                        pltpu.VMEM((tb, XBC_WIDTH), F32),
                        pltpu.VMEM((tb, LANES), F32),
                        pltpu.VMEM((tb, SSD_WIDTH), F32),
                        pltpu.VMEM((SSD_GROUPS, SSD_STATE, GROUP_WIDTH), F32)],
        compiler_params=_params("arbitrary", "arbitrary"),
        name="ssd",
    )(xbc, z, dt_raw, conv_w, conv_b, dt_bias, a_log, dskip_w, expand, g)


def _outproj_kernel(att_ref, ssd_ref, x_ref, mods_ref, g_ref, wa_ref, ws_ref, x1_ref, h_ref):
    mix = jnp.dot(att_ref[...], wa_ref[...], preferred_element_type=F32)
    mix = mix + jnp.dot(ssd_ref[...], ws_ref[...], preferred_element_type=F32)
    x1 = x_ref[...] + mods_ref[2:3, :] * mix
    x1_ref[...] = x1
    h = _rms(x1) * g_ref[...]
    h_ref[...] = (h * (1.0 + mods_ref[4:5, :]) + mods_ref[3:4, :]).astype(h_ref.dtype)


def _outproj(att, ssd, x, mods, g, w_att, w_ssd):
    b, s, d = x.shape
    tm = OUT_TM
    row = lambda bi, i: (bi, i, 0)
    return pl.pallas_call(
        _outproj_kernel,
        grid=(b, s // tm),
        in_specs=[pl.BlockSpec((None, tm, ATT_WIDTH), row),
                  pl.BlockSpec((None, tm, SSD_WIDTH), row),
                  pl.BlockSpec((None, tm, d), row),
                  pl.BlockSpec((None, N_MOD, d), lambda bi, i: (bi, 0, 0)),
                  _resident(g.shape), _resident(w_att.shape), _resident(w_ssd.shape)],
        out_specs=[pl.BlockSpec((None, tm, d), row), pl.BlockSpec((None, tm, d), row)],
        out_shape=[jax.ShapeDtypeStruct((b, s, d), F32), jax.ShapeDtypeStruct((b, s, d), BF16)],
        compiler_params=_params("arbitrary", "arbitrary"),
        name="outproj",
    )(att, ssd, x, mods, g, w_att, w_ssd)


def _ffn_kernel(h_ref, x1_ref, mods_ref, g_ref, wg_ref, wu_ref, wd_ref, o_ref, acc_s):
    k = pl.program_id(2)
    h = h_ref[...]
    gate = jnp.dot(h, wg_ref[...], preferred_element_type=F32)
    up = jnp.dot(h, wu_ref[...], preferred_element_type=F32)
    act = (_silu(gate) * up).astype(BF16)
    part = jnp.dot(act, wd_ref[...], preferred_element_type=F32)

    @pl.when(k == 0)
    def _():
        acc_s[...] = part

    @pl.when(k > 0)
    def _():
        acc_s[...] += part

    @pl.when(k == pl.num_programs(2) - 1)
    def _():
        x2 = x1_ref[...] + mods_ref[5:6, :] * acc_s[...]
        o_ref[...] = _rms(x2) * g_ref[...]


def _ffn(h, x1, mods, g, w_gate, w_up, w_down):
    b, s, d = x1.shape
    tm, th = FFN_TM, FFN_TH
    hid = w_gate.shape[1]
    row = lambda bi, i, k: (bi, i, 0)
    return pl.pallas_call(
        _ffn_kernel,
        grid=(b, s // tm, hid // th),
        in_specs=[pl.BlockSpec((None, tm, d), row),
                  pl.BlockSpec((None, tm, d), row),
                  pl.BlockSpec((None, N_MOD, d), lambda bi, i, k: (bi, 0, 0)),
                  pl.BlockSpec(g.shape, lambda bi, i, k: (0, 0)),
                  pl.BlockSpec((d, th), lambda bi, i, k: (0, k)),
                  pl.BlockSpec((d, th), lambda bi, i, k: (0, k)),
                  pl.BlockSpec((th, d), lambda bi, i, k: (k, 0))],
        out_specs=pl.BlockSpec((None, tm, d), row),
        out_shape=jax.ShapeDtypeStruct((b, s, d), F32),
        scratch_shapes=[pltpu.VMEM((tm, d), F32)],
        compiler_params=_params("arbitrary", "arbitrary", "arbitrary"),
        name="ffn",
    )(h, x1, mods, g, w_gate, w_up, w_down)


def _pad_lanes(v):
    return jnp.pad(v.astype(F32), (0, LANES - v.shape[0])).reshape(1, LANES)


def kernel(x, c, w_ada, b_ada, g_mix, w_in, rel_bias, conv_w, conv_b, dt_bias, a_log, d_skip,
           g_att_out, g_ssd_out, w_out, g_ffn, w_gate, w_up, w_down, g_final):
    b, s, d = x.shape
    depth = w_ada.shape[0]
    n_main = sum(_PROJ_SPLITS)
    head_of_lane = jnp.arange(SSD_WIDTH) // SSD_HEAD_DIM
    expand = (jnp.arange(LANES)[:, None] == head_of_lane[None, :]).astype(F32)
    c_pad = jnp.pad(c, ((0, 8 - b % 8 if b % 8 else 0), (0, 0)))
    for l in range(depth):
        mods = _adaln(c_pad, w_ada[l], b_ada[l].reshape(1, -1))[:b].reshape(b, N_MOD, d)
        w_main = w_in[l][:, :n_main].astype(BF16)
        w_dt = jnp.pad(w_in[l][:, n_main:], ((0, 0), (0, LANES - SSD_HEADS))).astype(BF16)
        q, k, v, z, xbc, dt_raw = _inproj(x, mods, g_mix[l].reshape(1, d), w_main, w_dt)
        att = _attention(q, k, v, _bias_table(rel_bias[l]), g_att_out[l].reshape(1, -1))
        ssd = _ssd(xbc, z, dt_raw, conv_w[l], conv_b[l].reshape(1, -1), _pad_lanes(dt_bias[l]),
                   _pad_lanes(a_log[l]), d_skip[l].astype(F32)[head_of_lane].reshape(1, -1), expand,
                   g_ssd_out[l].reshape(1, -1))
        w_o = w_out[l].astype(BF16)
        x1, h2 = _outproj(att, ssd, x, mods, g_ffn[l].reshape(1, d), w_o[:ATT_WIDTH], w_o[ATT_WIDTH:])
        last = l == depth - 1
        g_last = g_final.reshape(1, d) if last else None
        assert last, "final norm is fused into the last layer's FFN"
        x = _ffn(h2, x1, mods, g_last, w_gate[l].astype(BF16), w_up[l].astype(BF16),
                 w_down[l].astype(BF16))
    return x
```

```python
import functools

import jax
import jax.numpy as jnp
from jax import lax
from jax.experimental import pallas as pl
from jax.experimental.pallas import tpu as pltpu

F32 = jnp.float32
BF16 = jnp.bfloat16

D_MODEL = 2048
CHUNK = 64
LEFT_CHUNKS = 8
BAND = (LEFT_CHUNKS + 1) * CHUNK
ATT_HEADS = 16
ATT_HEAD_DIM = 64
ATT_WIDTH = ATT_HEADS * ATT_HEAD_DIM
REL_CLIP = 256
REL_FUTURE = CHUNK - 1
SSD_HEADS = 16
SSD_HEAD_DIM = 64
SSD_WIDTH = SSD_HEADS * SSD_HEAD_DIM
SSD_GROUPS = 2
SSD_STATE = 128
SSD_CONV = 4
XBC_WIDTH = SSD_WIDTH + 2 * SSD_GROUPS * SSD_STATE
GROUP_WIDTH = SSD_WIDTH // SSD_GROUPS
FFN_HIDDEN = 5632
N_MOD = 6
EPS = 1e-6

LANES = 128
MXU_WIDTH = 256
ATT_GROUP_WIDTH = MXU_WIDTH
ATT_GROUP_HEADS = ATT_GROUP_WIDTH // ATT_HEAD_DIM
ATT_GROUPS = ATT_WIDTH // ATT_GROUP_WIDTH
VMEM_LIMIT = 56 * 1024 * 1024

ADA_TN = 1024
PROJ_TM = 512
PROJ_TN = 512
ATT_TQ = 512
SSD_TB = 512
SSD_L = 64
OUT_TM = 512
FFN_TM = 1024
FFN_TH = 512
FFN_TN = 512

HIGHEST = lax.Precision.HIGHEST


def _params(*sem):
    return pltpu.CompilerParams(dimension_semantics=sem, vmem_limit_bytes=VMEM_LIMIT)


def _resident(shape):
    nd = len(shape)
    return pl.BlockSpec(shape, lambda *_: (0,) * nd, pipeline_mode=pl.Buffered(1))


def _rms(x):
    return x * lax.rsqrt(jnp.mean(x * x, axis=-1, keepdims=True) + EPS)


def _silu(x):
    h = 0.5 * x
    return h + h * jnp.tanh(h)


def _split_bf16(x, terms):
    parts = []
    for _ in range(terms):
        p = x.astype(BF16)
        parts.append(p)
        x = x - p.astype(F32)
    return parts


def _adaln_kernel(c_ref, w_ref, b_ref, o_ref):
    cond = _silu(c_ref[...])
    o_ref[...] = jnp.dot(cond.astype(BF16), w_ref[...].astype(BF16),
                         preferred_element_type=F32) + b_ref[...]


def _adaln(c_pad, w, b):
    rows, d = c_pad.shape
    n = w.shape[1]
    return pl.pallas_call(
        _adaln_kernel,
        grid=(n // ADA_TN,),
        in_specs=[pl.BlockSpec((rows, d), lambda j: (0, 0)),
                  pl.BlockSpec((d, ADA_TN), lambda j: (0, j)),
                  pl.BlockSpec((1, ADA_TN), lambda j: (0, j))],
        out_specs=pl.BlockSpec((rows, ADA_TN), lambda j: (0, j)),
        out_shape=jax.ShapeDtypeStruct((rows, n), F32),
        compiler_params=_params("arbitrary"),
        name="adaln",
    )(c_pad, w, b)


_PROJ_SPLITS = (ATT_WIDTH, ATT_WIDTH, ATT_WIDTH, SSD_WIDTH, XBC_WIDTH)


def _inproj_kernel(x_ref, mods_ref, g_ref, w_ref,
                   q_ref, k_ref, v_ref, z_ref, xbc_ref, dt_ref, h_s):
    x = x_ref[...]
    h = _rms(x) * g_ref[...]
    h = h * (1.0 + mods_ref[1:2, :]) + mods_ref[0:1, :]
    h_s[...] = h.astype(BF16)
    col = 0
    for ref, width in zip((q_ref, k_ref, v_ref, z_ref, xbc_ref), _PROJ_SPLITS):
        for j in range(0, width, PROJ_TN):
            ref[:, j:j + PROJ_TN] = jnp.dot(
                h_s[...], w_ref[:, col + j:col + j + PROJ_TN],
                preferred_element_type=F32).astype(ref.dtype)
        col += width
    dt_ref[...] = jnp.dot(h_s[...], w_ref[:, col:col + LANES], preferred_element_type=F32)


def _inproj(x, mods, g, w):
    b, s, d = x.shape
    tm = PROJ_TM
    row = lambda bi, i: (bi, i, 0)
    out_shape = [jax.ShapeDtypeStruct((b, s, w), BF16) for w in _PROJ_SPLITS]
    out_shape.append(jax.ShapeDtypeStruct((b, s, LANES), F32))
    out_specs = [pl.BlockSpec((None, tm, w), row) for w in _PROJ_SPLITS]
    out_specs.append(pl.BlockSpec((None, tm, LANES), row))
    return pl.pallas_call(
        _inproj_kernel,
        grid=(b, s // tm),
        in_specs=[pl.BlockSpec((None, tm, d), row),
                  pl.BlockSpec((None, N_MOD, d), lambda bi, i: (bi, 0, 0)),
                  _resident(g.shape), _resident(w.shape)],
        out_specs=out_specs,
        out_shape=out_shape,
        scratch_shapes=[pltpu.VMEM((tm, d), BF16)],
        compiler_params=_params("arbitrary", "arbitrary"),
        name="inproj",
    )(x, mods, g, w)


LOG2E = 1.4426950408889634


def _attn_kernel(q_ref, kp_ref, kc_ref, vp_ref, vc_ref, bias_ref, g_ref, o_ref, k_s, v_s):
    i = pl.program_id(1)
    tq = ATT_TQ
    k_s[0:tq, :] = kp_ref[...]
    k_s[tq:, :] = kc_ref[...]
    v_s[0:tq, :] = vp_ref[...]
    v_s[tq:, :] = vc_ref[...]

    gw = ATT_GROUP_WIDTH
    rr = lax.broadcasted_iota(jnp.int32, (gw, gw), 0) // CHUNK
    ll = lax.broadcasted_iota(jnp.int32, (gw, gw), 1) // ATT_HEAD_DIM
    own_head = rr == ll
    lane_head = lax.broadcasted_iota(jnp.int32, (CHUNK, gw), 1) // ATT_HEAD_DIM

    def chunk_body(masked, c, carry):
        q0 = pl.multiple_of(c * CHUNK, CHUNK)
        if masked:
            band_row = lax.broadcasted_iota(jnp.int32, (BAND, gw), 0)
            valid = band_row >= (LEFT_CHUNKS - c) * CHUNK
        outs = []
        for g in range(ATT_WIDTH // gw):
            lanes = slice(g * gw, (g + 1) * gw)
            qg = q_ref[pl.ds(q0, CHUNK), lanes]
            q_bd = jnp.where(own_head, jnp.concatenate([qg] * ATT_GROUP_HEADS, axis=0), jnp.zeros((), BF16))
            kb = k_s[pl.ds(q0, BAND), lanes]
            st = lax.dot_general(kb, q_bd, (((1,), (1,)), ((), ())),
                                 preferred_element_type=F32) + bias_ref[g]
            if masked:
                st = jnp.where(valid, st, -jnp.inf)
            m = jnp.max(st, axis=0, keepdims=True)
            e = jnp.exp2(st - m)
            den = jnp.sum(e, axis=0, keepdims=True)
            vb = v_s[pl.ds(q0, BAND), lanes]
            o4 = lax.dot_general(e.astype(BF16), vb, (((0,), (0,)), ((), ())),
                                 preferred_element_type=F32)
            inv_den = 1.0 / den
            out = None
            for half in range(gw // LANES):
                col = jnp.broadcast_to(inv_den[:, half * LANES:(half + 1) * LANES], (LANES, LANES)).T
                for j in range(LANES // CHUNK):
                    hj = half * (LANES // CHUNK) + j
                    rows = slice(hj * CHUNK, (hj + 1) * CHUNK)
                    blk = o4[rows, :] * jnp.concatenate([col[j * CHUNK:(j + 1) * CHUNK]] * (gw // LANES), axis=1)
                    out = blk if out is None else jnp.where(lane_head == hj, blk, out)
            outs.append(out)
        att = jnp.concatenate(outs, axis=-1)
        o_ref[pl.ds(q0, CHUNK), :] = (_rms(att) * g_ref[...]).astype(o_ref.dtype)
        return carry

    @pl.when(i == 0)
    def _():
        lax.fori_loop(0, tq // CHUNK, functools.partial(chunk_body, True), 0)

    @pl.when(i > 0)
    def _():
        lax.fori_loop(0, tq // CHUNK, functools.partial(chunk_body, False), 0)


def _attention(q, k, v, bias_t, g):
    b, s, w = q.shape
    tq = ATT_TQ
    assert tq == LEFT_CHUNKS * CHUNK
    cur = lambda bi, i: (bi, i, 0)
    prev = lambda bi, i: (bi, jnp.maximum(i - 1, 0), 0)
    blk = (None, tq, w)
    return pl.pallas_call(
        _attn_kernel,
        grid=(b, s // tq),
        in_specs=[pl.BlockSpec(blk, cur),
                  pl.BlockSpec(blk, prev), pl.BlockSpec(blk, cur),
                  pl.BlockSpec(blk, prev), pl.BlockSpec(blk, cur),
                  _resident(bias_t.shape), _resident(g.shape)],
        out_specs=pl.BlockSpec(blk, cur),
        out_shape=jax.ShapeDtypeStruct((b, s, w), BF16),
        scratch_shapes=[pltpu.VMEM((2 * tq, w), BF16), pltpu.VMEM((2 * tq, w), BF16)],
        compiler_params=_params("arbitrary", "arbitrary"),
        name="chunk_attn",
    )(q, k, k, v, v, bias_t, g)


def _bias_table(rel_bias):
    n_rel = rel_bias.shape[1]
    n_f = BAND + REL_FUTURE
    f = jnp.concatenate([jnp.broadcast_to(rel_bias[:, n_rel - 1:], (ATT_HEADS, n_f - n_rel)),
                         rel_bias[:, ::-1]], axis=1).astype(F32) * LOG2E
    skew = jnp.broadcast_to(jnp.pad(f, ((0, 0), (0, 1)))[:, None, :], (ATT_HEADS, CHUNK, n_f + 1))
    skew = skew.reshape(ATT_HEADS, CHUNK * (n_f + 1))[:, :CHUNK * n_f].reshape(ATT_HEADS, CHUNK, n_f)
    bias = skew[:, :, REL_FUTURE:]
    bias = bias.reshape(ATT_GROUPS, ATT_GROUP_HEADS, CHUNK, BAND)
    return jnp.transpose(bias, (0, 3, 1, 2)).reshape(ATT_GROUPS, BAND, ATT_GROUP_WIDTH)


_HALO = 8


def _ssd_kernel(xbc_ref, z_ref, dt_ref, convw_ref, convb_ref, dtb_ref, a_ref, dskip_ref,
                expand_ref, g_ref, o_ref, raw_s, act_s, dt_s, y_s, h_s):
    i = pl.program_id(1)
    tb = SSD_TB
    L = SSD_L

    @pl.when(i == 0)
    def _():
        raw_s[0:_HALO, :] = jnp.zeros((_HALO, XBC_WIDTH), F32)
        h_s[...] = jnp.zeros_like(h_s)

    @pl.when(i > 0)
    def _():
        raw_s[0:_HALO, :] = raw_s[tb:tb + _HALO, :]

    raw_s[_HALO:, :] = xbc_ref[...].astype(F32)
    raw = raw_s[...]
    conv = convb_ref[...] + convw_ref[SSD_CONV - 1:SSD_CONV, :] * raw[_HALO:]
    for back in range(1, SSD_CONV):
        conv = conv + convw_ref[SSD_CONV - 1 - back:SSD_CONV - back, :] * pltpu.roll(raw, back, axis=0)[_HALO:]
    act_s[...] = _silu(conv)

    dtr = dt_ref[...] + dtb_ref[...]
    dt_s[...] = jnp.maximum(dtr, 0.0) + jnp.log1p(jnp.exp(-jnp.abs(dtr)))

    a = -jnp.exp(a_ref[...])
    ri = lax.broadcasted_iota(jnp.int32, (L, L), 0)
    ci = lax.broadcasted_iota(jnp.int32, (L, L), 1)
    causal = ri >= ci
    tril = causal.astype(F32)
    low_lanes = lax.broadcasted_iota(jnp.int32, (L, LANES), 1) < SSD_HEAD_DIM
    expand = expand_ref[...]
    dskip = dskip_ref[...]
    bc0 = SSD_WIDTH
    cc0 = SSD_WIDTH + SSD_GROUPS * SSD_STATE

    def chunk_body(c, carry):
        r0 = pl.multiple_of(c * L, L)
        dtc = dt_s[pl.ds(r0, L), :]
        adt = dtc * a
        cs = jnp.dot(tril, adt, precision=HIGHEST, preferred_element_type=F32)
        cs_t = cs.T
        last = cs[L - 1:L, :]
        stacked = jnp.concatenate([dtc, jnp.exp(last - cs), jnp.exp(cs)], axis=0)
        wide = jnp.dot(jnp.concatenate(_split_bf16(stacked, 2), axis=1), expand,
                       preferred_element_type=F32)
        dt_w, decay_w, ecs_w = wide[0:L], wide[L:2 * L], wide[2 * L:3 * L]
        cdec = jnp.broadcast_to(jnp.exp(last), (8, LANES))
        cdec_w = sum(jnp.dot(part, expand[:LANES], preferred_element_type=F32)
                     for part in _split_bf16(cdec, 3))[0:1]

        xs = act_s[pl.ds(r0, L), 0:SSD_WIDTH]
        xdt = xs * dt_w
        xdt_b = xdt.astype(BF16)
        xdec_b = (xdt * decay_w).astype(BF16)
        ys = []
        for g in range(SSD_GROUPS):
            gl = slice(g * GROUP_WIDTH, (g + 1) * GROUP_WIDTH)
            bm = act_s[pl.ds(r0, L), bc0 + g * SSD_STATE:bc0 + (g + 1) * SSD_STATE].astype(BF16)
            cm = act_s[pl.ds(r0, L), cc0 + g * SSD_STATE:cc0 + (g + 1) * SSD_STATE].astype(BF16)
            cb = lax.dot_general(cm, bm, (((1,), (1,)), ((), ())), preferred_element_type=F32)
            hprev = h_s[g]
            y_off = jnp.dot(cm, hprev.astype(BF16), preferred_element_type=F32) * ecs_w[:, gl]
            states = lax.dot_general(bm, xdec_b[:, gl], (((0,), (0,)), ((), ())),
                                     preferred_element_type=F32)
            h_s[g] = hprev * cdec_w[:, gl] + states
            yd = []
            for pr in range(GROUP_WIDTH // LANES):
                hp = g * (GROUP_WIDTH // LANES) + pr
                xp = xdt_b[:, hp * LANES:(hp + 1) * LANES]
                halves = []
                for hh in (2 * hp, 2 * hp + 1):
                    seg = jnp.exp(jnp.where(causal, cs[:, hh:hh + 1] - cs_t[hh:hh + 1, :], -jnp.inf))
                    halves.append(jnp.dot((cb * seg).astype(BF16), xp, preferred_element_type=F32))
                yd.append(jnp.where(low_lanes, halves[0], halves[1]))
            ys.append(jnp.concatenate(yd, axis=-1) + y_off)
        y = jnp.concatenate(ys, axis=-1) + xs * dskip
        y_s[pl.ds(r0, L), :] = y
        return carry

    lax.fori_loop(0, tb // L, chunk_body, 0)
    gated = y_s[...] * _silu(z_ref[...].astype(F32))
    o_ref[...] = (_rms(gated) * g_ref[...]).astype(o_ref.dtype)


def _ssd(xbc, z, dt_raw, conv_w, conv_b, dt_bias, a_log, dskip_w, expand, g):
    b, s, _ = xbc.shape
    tb = SSD_TB
    row = lambda bi, i: (bi, i, 0)
    return pl.pallas_call(
        _ssd_kernel,
        grid=(b, s // tb),
        in_specs=[pl.BlockSpec((None, tb, XBC_WIDTH), row),
                  pl.BlockSpec((None, tb, SSD_WIDTH), row),
                  pl.BlockSpec((None, tb, LANES), row),
                  _resident(conv_w.shape), _resident(conv_b.shape), _resident(dt_bias.shape),
                  _resident(a_log.shape), _resident(dskip_w.shape), _resident(expand.shape),
                  _resident(g.shape)],
        out_specs=pl.BlockSpec((None, tb, SSD_WIDTH), row),
        out_shape=jax.ShapeDtypeStruct((b, s, SSD_WIDTH), BF16),
        scratch_shapes=[pltpu.VMEM((_HALO + tb, XBC_WIDTH), F32),
                        pltpu.VMEM((tb, XBC_WIDTH), F32),
                        pltpu.VMEM((tb, LANES), F32),
                        pltpu.VMEM((tb, SSD_WIDTH), F32),
                        pltpu.VMEM((SSD_GROUPS, SSD_STATE, GROUP_WIDTH), F32)],
        compiler_params=_params("arbitrary", "arbitrary"),
        name="ssd",
    )(xbc, z, dt_raw, conv_w, conv_b, dt_bias, a_log, dskip_w, expand, g)


def _outproj_kernel(att_ref, ssd_ref, x_ref, mods_ref, g_ref, w_ref, x1_ref, h_ref):
    mix = jnp.dot(att_ref[...], w_ref[0:ATT_WIDTH, :], preferred_element_type=F32)
    mix = mix + jnp.dot(ssd_ref[...], w_ref[ATT_WIDTH:, :], preferred_element_type=F32)
    x1 = x_ref[...] + mods_ref[2:3, :] * mix
    x1_ref[...] = x1
    h = _rms(x1) * g_ref[...]
    h_ref[...] = (h * (1.0 + mods_ref[4:5, :]) + mods_ref[3:4, :]).astype(h_ref.dtype)


def _outproj(att, ssd, x, mods, g, w):
    b, s, d = x.shape
    tm = OUT_TM
    row = lambda bi, i: (bi, i, 0)
    return pl.pallas_call(
        _outproj_kernel,
        grid=(b, s // tm),
        in_specs=[pl.BlockSpec((None, tm, ATT_WIDTH), row),
                  pl.BlockSpec((None, tm, SSD_WIDTH), row),
                  pl.BlockSpec((None, tm, d), row),
                  pl.BlockSpec((None, N_MOD, d), lambda bi, i: (bi, 0, 0)),
                  _resident(g.shape), _resident(w.shape)],
        out_specs=[pl.BlockSpec((None, tm, d), row), pl.BlockSpec((None, tm, d), row)],
        out_shape=[jax.ShapeDtypeStruct((b, s, d), F32), jax.ShapeDtypeStruct((b, s, d), BF16)],
        compiler_params=_params("arbitrary", "arbitrary"),
        name="outproj",
    )(att, ssd, x, mods, g, w)


def _ffn_kernel(final_norm, h_ref, x1_ref, mods_ref, g_ref, wg_ref, wu_ref, wd_ref, o_ref):
    k = pl.program_id(2)

    @pl.when(k == 0)
    def _():
        o_ref[...] = x1_ref[...]

    h = h_ref[...]
    gate = jnp.dot(h, wg_ref[...], preferred_element_type=F32)
    up = jnp.dot(h, wu_ref[...], preferred_element_type=F32)
    act = (_silu(gate) * up).astype(BF16)
    d = o_ref.shape[-1]
    for n0 in range(0, d, FFN_TN):
        cols = slice(n0, n0 + FFN_TN)
        o_ref[:, cols] += mods_ref[5:6, cols] * jnp.dot(act, wd_ref[:, cols], preferred_element_type=F32)

    if final_norm:
        @pl.when(k == pl.num_programs(2) - 1)
        def _():
            o_ref[...] = _rms(o_ref[...]) * g_ref[...]


def _ffn(h, x1, mods, g_final, w_gate, w_up, w_down):
    b, s, d = x1.shape
    tm, th = FFN_TM, FFN_TH
    hid = w_gate.shape[1]
    row = lambda bi, i, k: (bi, i, 0)
    final_norm = g_final is not None
    g = g_final if final_norm else jnp.ones((1, d), F32)
    return pl.pallas_call(
        functools.partial(_ffn_kernel, final_norm),
        grid=(b, s // tm, hid // th),
        in_specs=[pl.BlockSpec((None, tm, d), row),
                  pl.BlockSpec((None, tm, d), row, pipeline_mode=pl.Buffered(1)),
                  pl.BlockSpec((None, N_MOD, d), lambda bi, i, k: (bi, 0, 0)),
                  pl.BlockSpec(g.shape, lambda bi, i, k: (0, 0)),
                  pl.BlockSpec((d, th), lambda bi, i, k: (0, k)),
                  pl.BlockSpec((d, th), lambda bi, i, k: (0, k)),
                  pl.BlockSpec((th, d), lambda bi, i, k: (k, 0))],
        out_specs=pl.BlockSpec((None, tm, d), row),
        out_shape=jax.ShapeDtypeStruct((b, s, d), F32),
        compiler_params=_params("arbitrary", "arbitrary", "arbitrary"),
        name="ffn",
    )(h, x1, mods, g, w_gate, w_up, w_down)


def _pad_lanes(v):
    return jnp.pad(v.astype(F32), (0, LANES - v.shape[0])).reshape(1, LANES)


def kernel(x, c, w_ada, b_ada, g_mix, w_in, rel_bias, conv_w, conv_b, dt_bias, a_log, d_skip,
           g_att_out, g_ssd_out, w_out, g_ffn, w_gate, w_up, w_down, g_final):
    b, s, d = x.shape
    depth = w_ada.shape[0]
    n_main = sum(_PROJ_SPLITS)
    head_of_lane = jnp.arange(SSD_WIDTH) // SSD_HEAD_DIM
    expand = (jnp.arange(2 * LANES)[:, None] % LANES == head_of_lane[None, :]).astype(BF16)
    c_pad = jnp.pad(c, ((0, 8 - b % 8 if b % 8 else 0), (0, 0)))
    for l in range(depth):
        mods = _adaln(c_pad, w_ada[l], b_ada[l].reshape(1, -1))[:b].reshape(b, N_MOD, d)
        n_in = w_in.shape[-1]
        q_scale = jnp.where(jnp.arange(n_in) < ATT_WIDTH, ATT_HEAD_DIM ** -0.5 * LOG2E, 1.0).astype(F32)
        w_proj = jnp.pad((w_in[l] * q_scale).astype(BF16), ((0, 0), (0, n_main + LANES - n_in)))
        q, k, v, z, xbc, dt_raw = _inproj(x, mods, g_mix[l].reshape(1, d), w_proj)
        att = _attention(q, k, v, _bias_table(rel_bias[l]), g_att_out[l].reshape(1, -1))
        ssd = _ssd(xbc, z, dt_raw, conv_w[l], conv_b[l].reshape(1, -1), _pad_lanes(dt_bias[l]),
                   _pad_lanes(a_log[l]), d_skip[l].astype(F32)[head_of_lane].reshape(1, -1), expand,
                   g_ssd_out[l].reshape(1, -1))
        x1, h2 = _outproj(att, ssd, x, mods, g_ffn[l].reshape(1, d), w_out[l].astype(BF16))
        g_last = g_final.reshape(1, d) if l == depth - 1 else None
        x = _ffn(h2, x1, mods, g_last, w_gate[l].astype(BF16), w_up[l].astype(BF16),
                 w_down[l].astype(BF16))
    return x
```

```python
import functools

import jax
import jax.numpy as jnp
from jax import lax
from jax.experimental import pallas as pl
from jax.experimental.pallas import tpu as pltpu

F32 = jnp.float32
BF16 = jnp.bfloat16

D_MODEL = 2048
CHUNK = 64
LEFT_CHUNKS = 8
BAND = (LEFT_CHUNKS + 1) * CHUNK
ATT_HEADS = 16
ATT_HEAD_DIM = 64
ATT_WIDTH = ATT_HEADS * ATT_HEAD_DIM
REL_CLIP = 256
REL_FUTURE = CHUNK - 1
SSD_HEADS = 16
SSD_HEAD_DIM = 64
SSD_WIDTH = SSD_HEADS * SSD_HEAD_DIM
SSD_GROUPS = 2
SSD_STATE = 128
SSD_CONV = 4
XBC_WIDTH = SSD_WIDTH + 2 * SSD_GROUPS * SSD_STATE
GROUP_WIDTH = SSD_WIDTH // SSD_GROUPS
FFN_HIDDEN = 5632
N_MOD = 6
EPS = 1e-6

LANES = 128
MXU_WIDTH = 256
ATT_GROUP_WIDTH = MXU_WIDTH
ATT_GROUP_HEADS = ATT_GROUP_WIDTH // ATT_HEAD_DIM
ATT_GROUPS = ATT_WIDTH // ATT_GROUP_WIDTH
BAND_PAD = -(-BAND // LANES) * LANES
VMEM_LIMIT = 56 * 1024 * 1024

ADA_TN = 1024
PROJ_TM = 512
PROJ_TN = 512
ATT_TQ = 512
SSD_TB = 512
SSD_L = 128
OUT_TM = 512
FFN_TM = 1024
FFN_TH = 512
FFN_TN = 512

HIGHEST = lax.Precision.HIGHEST


def _params(*sem):
    return pltpu.CompilerParams(dimension_semantics=sem, vmem_limit_bytes=VMEM_LIMIT)


def _resident(shape):
    nd = len(shape)
    return pl.BlockSpec(shape, lambda *_: (0,) * nd, pipeline_mode=pl.Buffered(1))


def _rms(x):
    return x * lax.rsqrt(jnp.mean(x * x, axis=-1, keepdims=True) + EPS)


def _silu(x):
    h = 0.5 * x
    return h + h * jnp.tanh(h)


def _split_bf16(x, terms):
    parts = []
    for _ in range(terms):
        p = x.astype(BF16)
        parts.append(p)
        x = x - p.astype(F32)
    return parts


def _adaln_kernel(c_ref, w_ref, b_ref, o_ref):
    cond = _silu(c_ref[...])
    o_ref[...] = jnp.dot(cond.astype(BF16), w_ref[...].astype(BF16),
                         preferred_element_type=F32) + b_ref[...]


def _adaln(c_pad, w, b):
    rows, d = c_pad.shape
    n = w.shape[1]
    return pl.pallas_call(
        _adaln_kernel,
        grid=(n // ADA_TN,),
        in_specs=[pl.BlockSpec((rows, d), lambda j: (0, 0)),
                  pl.BlockSpec((d, ADA_TN), lambda j: (0, j)),
                  pl.BlockSpec((1, ADA_TN), lambda j: (0, j))],
        out_specs=pl.BlockSpec((rows, ADA_TN), lambda j: (0, j)),
        out_shape=jax.ShapeDtypeStruct((rows, n), F32),
        compiler_params=_params("arbitrary"),
        name="adaln",
    )(c_pad, w, b)


_PROJ_SPLITS = (ATT_WIDTH, ATT_WIDTH, ATT_WIDTH, SSD_WIDTH, XBC_WIDTH)


def _inproj_kernel(x_ref, mods_ref, g_ref, w_ref, wdt_ref,
                   q_ref, k_ref, v_ref, z_ref, xbc_ref, dt_ref, h_s):
    x = x_ref[...]
    h = _rms(x) * g_ref[...]
    h = h * (1.0 + mods_ref[1:2, :]) + mods_ref[0:1, :]
    h_s[...] = h.astype(BF16)
    col = 0
    for ref, width in zip((q_ref, k_ref, v_ref, z_ref, xbc_ref), _PROJ_SPLITS):
        for j in range(0, width, PROJ_TN):
            ref[:, j:j + PROJ_TN] = jnp.dot(
                h_s[...], w_ref[:, col + j:col + j + PROJ_TN],
                preferred_element_type=F32).astype(ref.dtype)
        col += width
    dt_ref[...] = jnp.dot(h_s[...], wdt_ref[...], preferred_element_type=F32)


def _inproj(x, mods, g, w, w_dt):
    b, s, d = x.shape
    tm = PROJ_TM
    row = lambda bi, i: (bi, i, 0)
    out_shape = [jax.ShapeDtypeStruct((b, s, w), BF16) for w in _PROJ_SPLITS]
    out_shape.append(jax.ShapeDtypeStruct((b, s, LANES), F32))
    out_specs = [pl.BlockSpec((None, tm, w), row) for w in _PROJ_SPLITS]
    out_specs.append(pl.BlockSpec((None, tm, LANES), row))
    return pl.pallas_call(
        _inproj_kernel,
        grid=(b, s // tm),
        in_specs=[pl.BlockSpec((None, tm, d), row),
                  pl.BlockSpec((None, N_MOD, d), lambda bi, i: (bi, 0, 0)),
                  _resident(g.shape), _resident(w.shape), _resident(w_dt.shape)],
        out_specs=out_specs,
        out_shape=out_shape,
        scratch_shapes=[pltpu.VMEM((tm, d), BF16)],
        compiler_params=_params("arbitrary", "arbitrary"),
        name="inproj",
    )(x, mods, g, w, w_dt)


LOG2E = 1.4426950408889634


def _attn_kernel(q_ref, kp_ref, kc_ref, vp_ref, vc_ref, bias_ref, g_ref, o_ref,
                 k_s, v_s, s_s, pt_s, max_s, den_s, o_s, inv_s):
    i = pl.program_id(1)
    tq = ATT_TQ
    k_s[0:tq, :] = kp_ref[...]
    k_s[tq:, :] = kc_ref[...]
    v_s[0:tq, :] = vp_ref[...]
    v_s[tq:2 * tq, :] = vc_ref[...]
    v_s[2 * tq:, :] = jnp.zeros((BAND_PAD - BAND, ATT_WIDTH), BF16)

    gw = ATT_GROUP_WIDTH
    rr = lax.broadcasted_iota(jnp.int32, (gw, gw), 0) // CHUNK
    ll = lax.broadcasted_iota(jnp.int32, (gw, gw), 1) // ATT_HEAD_DIM
    own_head = rr == ll
    lane_head = lax.broadcasted_iota(jnp.int32, (CHUNK, gw), 1) // ATT_HEAD_DIM
    groups = range(ATT_GROUPS)
    n_chunks = tq // CHUNK

    def scores_stage(c, slot):
        q0 = c * CHUNK if isinstance(c, int) else pl.multiple_of(c * CHUNK, CHUNK)
        for g in groups:
            lanes = slice(g * gw, (g + 1) * gw)
            qg = q_ref[pl.ds(q0, CHUNK), lanes]
            q_bd = jnp.where(own_head, jnp.concatenate([qg] * ATT_GROUP_HEADS, axis=0), jnp.zeros((), BF16))
            kb = k_s[pl.ds(q0, BAND), lanes]
            st = lax.dot_general(kb, q_bd, (((1,), (1,)), ((), ())),
                                 preferred_element_type=F32) + bias_ref[g]
            s_s[slot, g] = st
            max_s[slot, g] = jnp.broadcast_to(jnp.max(st, axis=0, keepdims=True), (8, gw))

    def mask_stage(c, slot):
        n_before = jnp.maximum(LEFT_CHUNKS - (i * n_chunks + c), 0)

        def mask_slab(j, carry):
            r0 = pl.multiple_of(j * CHUNK, CHUNK)
            for g in groups:
                s_s[slot, g, pl.ds(r0, CHUNK), :] = jnp.full((CHUNK, gw), -jnp.inf, F32)
            return carry

        lax.fori_loop(0, n_before, mask_slab, 0)

        @pl.when(n_before > 0)
        def _():
            for g in groups:
                max_s[slot, g] = jnp.broadcast_to(jnp.max(s_s[slot, g], axis=0, keepdims=True), (8, gw))

    def softmax_stage(slot):
        for g in groups:
            e = jnp.exp2(s_s[slot, g] - max_s[slot, g, 0:1, :])
            den_s[slot, g] = jnp.broadcast_to(jnp.sum(e, axis=0, keepdims=True), (8, gw))
            eb = jnp.concatenate([e.astype(BF16), jnp.zeros((BAND_PAD - BAND, gw), BF16)], axis=0)
            for r in range(0, BAND_PAD, LANES):
                pt_s[slot, g, :, r:r + LANES] = eb[r:r + LANES, :].T

    def pv_stage(c, slot):
        q0 = c * CHUNK if isinstance(c, int) else pl.multiple_of(c * CHUNK, CHUNK)
        for g in groups:
            lanes = slice(g * gw, (g + 1) * gw)
            vb = v_s[pl.ds(q0, BAND_PAD), lanes]
            o4 = jnp.dot(pt_s[slot, g], vb, preferred_element_type=F32)
            out = o4[0:CHUNK]
            for hj in range(1, ATT_GROUP_HEADS):
                out = jnp.where(lane_head == hj, o4[hj * CHUNK:(hj + 1) * CHUNK], out)
            o_s[slot, :, lanes] = out
            inv_s[slot, g] = 1.0 / den_s[slot, g]

    def finalize_stage(c, slot):
        q0 = c * CHUNK if isinstance(c, int) else pl.multiple_of(c * CHUNK, CHUNK)
        scales = []
        for g in groups:
            inv_den = inv_s[slot, g, 0:1, :]
            scale = None
            for half in range(gw // LANES):
                col = jnp.broadcast_to(inv_den[:, half * LANES:(half + 1) * LANES], (LANES, LANES)).T
                for j in range(LANES // CHUNK):
                    hj = half * (LANES // CHUNK) + j
                    blk = jnp.concatenate([col[j * CHUNK:(j + 1) * CHUNK]] * (gw // LANES), axis=1)
                    scale = blk if scale is None else jnp.where(lane_head == hj, blk, scale)
            scales.append(scale)
        att = o_s[slot] * jnp.concatenate(scales, axis=-1)
        o_ref[pl.ds(q0, CHUNK), :] = (_rms(att) * g_ref[...]).astype(o_ref.dtype)

    def block(c, slot, first=0, last=n_chunks - 1):
        if first <= c <= last:
            scores_stage(c, slot)
        if first <= c - 1 <= last:
            softmax_stage(1 - slot)
        if first <= c - 2 <= last:
            pv_stage(c - 2, slot)
        if first <= c - 3 <= last:
            finalize_stage(c - 3, 1 - slot)
        if first <= c <= last:
            mask_stage(c, slot)

    def steady_pair(j, carry):
        c = 3 + 2 * j
        for k in range(2):
            scores_stage(c + k, (1 + k) % 2)
            softmax_stage(k % 2)
            pv_stage(c + k - 2, (1 + k) % 2)
            finalize_stage(c + k - 3, k % 2)
            mask_stage(c + k, (1 + k) % 2)
        return carry

    n_pairs = (n_chunks - 4) // 2
    assert n_chunks >= 4 and n_chunks % 2 == 0
    for c in range(3):
        block(c, c % 2)
    lax.fori_loop(0, n_pairs, steady_pair, 0)
    for c in range(3 + 2 * n_pairs, n_chunks + 3):
        block(c, c % 2)


def _attention(q, k, v, bias_t, g):
    b, s, w = q.shape
    tq = ATT_TQ
    assert tq == LEFT_CHUNKS * CHUNK
    cur = lambda bi, i: (bi, i, 0)
    prev = lambda bi, i: (bi, jnp.maximum(i - 1, 0), 0)
    blk = (None, tq, w)
    return pl.pallas_call(
        _attn_kernel,
        grid=(b, s // tq),
        in_specs=[pl.BlockSpec(blk, cur),
                  pl.BlockSpec(blk, prev), pl.BlockSpec(blk, cur),
                  pl.BlockSpec(blk, prev), pl.BlockSpec(blk, cur),
                  _resident(bias_t.shape), _resident(g.shape)],
        out_specs=pl.BlockSpec(blk, cur),
        out_shape=jax.ShapeDtypeStruct((b, s, w), BF16),
        scratch_shapes=[pltpu.VMEM((2 * tq, w), BF16),
                        pltpu.VMEM((2 * tq + BAND_PAD - BAND, w), BF16),
                        pltpu.VMEM((2, ATT_GROUPS, BAND, ATT_GROUP_WIDTH), F32),
                        pltpu.VMEM((2, ATT_GROUPS, ATT_GROUP_WIDTH, BAND_PAD), BF16),
                        pltpu.VMEM((2, ATT_GROUPS, 8, ATT_GROUP_WIDTH), F32),
                        pltpu.VMEM((2, ATT_GROUPS, 8, ATT_GROUP_WIDTH), F32),
                        pltpu.VMEM((2, CHUNK, w), F32),
                        pltpu.VMEM((2, ATT_GROUPS, 8, ATT_GROUP_WIDTH), F32)],
        compiler_params=_params("arbitrary", "arbitrary"),
        name="chunk_attn",
    )(q, k, k, v, v, bias_t, g)


def _bias_table(rel_bias):
    n_rel = rel_bias.shape[1]
    n_f = BAND + REL_FUTURE
    f = jnp.concatenate([jnp.broadcast_to(rel_bias[:, n_rel - 1:], (ATT_HEADS, n_f - n_rel)),
                         rel_bias[:, ::-1]], axis=1).astype(F32) * LOG2E
    skew = jnp.broadcast_to(jnp.pad(f, ((0, 0), (0, 1)))[:, None, :], (ATT_HEADS, CHUNK, n_f + 1))
    skew = skew.reshape(ATT_HEADS, CHUNK * (n_f + 1))[:, :CHUNK * n_f].reshape(ATT_HEADS, CHUNK, n_f)
    bias = skew[:, :, REL_FUTURE:]
    bias = bias.reshape(ATT_GROUPS, ATT_GROUP_HEADS, CHUNK, BAND)
    return jnp.transpose(bias, (0, 3, 1, 2)).reshape(ATT_GROUPS, BAND, ATT_GROUP_WIDTH)


_HALO = 8


def _ssd_kernel(xbc_ref, z_ref, dt_ref, convw_ref, convb_ref, dtb_ref, a_ref, dskip_ref,
                expand_ref, g_ref, o_ref, raw_s, act_s, dt_s, y_s, h_s):
    i = pl.program_id(1)
    tb = SSD_TB
    L = SSD_L

    @pl.when(i == 0)
    def _():
        raw_s[0:_HALO, :] = jnp.zeros((_HALO, XBC_WIDTH), F32)
        h_s[...] = jnp.zeros_like(h_s)

    @pl.when(i > 0)
    def _():
        raw_s[0:_HALO, :] = raw_s[tb:tb + _HALO, :]

    raw_s[_HALO:, :] = xbc_ref[...].astype(F32)
    assert SSD_CONV == 4
    raw = raw_s[...]
    back1 = pltpu.roll(raw, 1, axis=0)
    near = convw_ref[3:4, :] * raw + convw_ref[2:3, :] * back1
    far = convw_ref[1:2, :] * raw + convw_ref[0:1, :] * back1
    conv = convb_ref[...] + near[_HALO:] + pltpu.roll(far, 2, axis=0)[_HALO:]
    act_s[...] = _silu(conv)

    dtr = dt_ref[...] + dtb_ref[...]
    dt_s[...] = jnp.maximum(dtr, 0.0) + jnp.log1p(jnp.exp(-jnp.abs(dtr)))

    a = -jnp.exp(a_ref[...])
    ri = lax.broadcasted_iota(jnp.int32, (L, L), 0)
    ci = lax.broadcasted_iota(jnp.int32, (L, L), 1)
    causal = ri >= ci
    tril = causal.astype(BF16)
    low_lanes = lax.broadcasted_iota(jnp.int32, (L, LANES), 1) < SSD_HEAD_DIM
    expand = expand_ref[...]
    dskip = dskip_ref[...]
    bc0 = SSD_WIDTH
    cc0 = SSD_WIDTH + SSD_GROUPS * SSD_STATE

    def chunk_body(c, carry):
        r0 = pl.multiple_of(c * L, L)
        dtc = dt_s[pl.ds(r0, L), :]
        adt = dtc * a
        cs3 = jnp.dot(tril, jnp.concatenate(_split_bf16(adt, 3), axis=1), preferred_element_type=F32)
        cs = cs3[:, 0:LANES] + cs3[:, LANES:2 * LANES] + cs3[:, 2 * LANES:]
        cs_t = cs.T
        last = cs[L - 1:L, :]
        stacked = jnp.concatenate([dtc, jnp.exp(last - cs), jnp.exp(cs)], axis=0)
        wide = jnp.dot(jnp.concatenate(_split_bf16(stacked, 2), axis=1), expand,
                       preferred_element_type=F32)
        dt_w, decay_w, ecs_w = wide[0:L], wide[L:2 * L], wide[2 * L:3 * L]
        cdec = jnp.broadcast_to(jnp.exp(last), (8, LANES))
        cdec_w = sum(jnp.dot(part, expand[:LANES], preferred_element_type=F32)
                     for part in _split_bf16(cdec, 3))[0:1]

        xs = act_s[pl.ds(r0, L), 0:SSD_WIDTH]
        xdt = xs * dt_w
        xdt_b = xdt.astype(BF16)
        xdec_b = (xdt * decay_w).astype(BF16)
        ys = []
        for g in range(SSD_GROUPS):
            gl = slice(g * GROUP_WIDTH, (g + 1) * GROUP_WIDTH)
            bm = act_s[pl.ds(r0, L), bc0 + g * SSD_STATE:bc0 + (g + 1) * SSD_STATE].astype(BF16)
            cm = act_s[pl.ds(r0, L), cc0 + g * SSD_STATE:cc0 + (g + 1) * SSD_STATE].astype(BF16)
            cb = lax.dot_general(cm, bm, (((1,), (1,)), ((), ())), preferred_element_type=F32)
            hprev = h_s[g]
            y_off = jnp.dot(cm, hprev.astype(BF16), preferred_element_type=F32) * ecs_w[:, gl]
            states = lax.dot_general(bm, xdec_b[:, gl], (((0,), (0,)), ((), ())),
                                     preferred_element_type=F32)
            h_s[g] = hprev * cdec_w[:, gl] + states
            yd = []
            for pr in range(GROUP_WIDTH // LANES):
                hp = g * (GROUP_WIDTH // LANES) + pr
                xp = xdt_b[:, hp * LANES:(hp + 1) * LANES]
                halves = []
                for hh in (2 * hp, 2 * hp + 1):
                    seg = jnp.exp(jnp.where(causal, cs[:, hh:hh + 1] - cs_t[hh:hh + 1, :], -jnp.inf))
                    halves.append(jnp.dot((cb * seg).astype(BF16), xp, preferred_element_type=F32))
                yd.append(jnp.where(low_lanes, halves[0], halves[1]))
            ys.append(jnp.concatenate(yd, axis=-1) + y_off)
        y = jnp.concatenate(ys, axis=-1) + xs * dskip
        y_s[pl.ds(r0, L), :] = y
        return carry

    lax.fori_loop(0, tb // L, chunk_body, 0)
    gated = y_s[...] * _silu(z_ref[...].astype(F32))
    o_ref[...] = (_rms(gated) * g_ref[...]).astype(o_ref.dtype)


def _ssd(xbc, z, dt_raw, conv_w, conv_b, dt_bias, a_log, dskip_w, expand, g):
    b, s, _ = xbc.shape
    tb = SSD_TB
    row = lambda bi, i: (bi, i, 0)
    return pl.pallas_call(
        _ssd_kernel,
        grid=(b, s // tb),
        in_specs=[pl.BlockSpec((None, tb, XBC_WIDTH), row),
                  pl.BlockSpec((None, tb, SSD_WIDTH), row),
                  pl.BlockSpec((None, tb, LANES), row),
                  _resident(conv_w.shape), _resident(conv_b.shape), _resident(dt_bias.shape),
                  _resident(a_log.shape), _resident(dskip_w.shape), _resident(expand.shape),
                  _resident(g.shape)],
        out_specs=pl.BlockSpec((None, tb, SSD_WIDTH), row),
        out_shape=jax.ShapeDtypeStruct((b, s, SSD_WIDTH), BF16),
        scratch_shapes=[pltpu.VMEM((_HALO + tb, XBC_WIDTH), F32),
                        pltpu.VMEM((tb, XBC_WIDTH), F32),
                        pltpu.VMEM((tb, LANES), F32),
                        pltpu.VMEM((tb, SSD_WIDTH), F32),
                        pltpu.VMEM((SSD_GROUPS, SSD_STATE, GROUP_WIDTH), F32)],
        compiler_params=_params("arbitrary", "arbitrary"),
        name="ssd",
    )(xbc, z, dt_raw, conv_w, conv_b, dt_bias, a_log, dskip_w, expand, g)


def _outproj_kernel(att_ref, ssd_ref, x_ref, mods_ref, g_ref, w_ref, x1_ref, h_ref):
    mix = jnp.dot(att_ref[...], w_ref[0:ATT_WIDTH, :], preferred_element_type=F32)
    mix = mix + jnp.dot(ssd_ref[...], w_ref[ATT_WIDTH:, :], preferred_element_type=F32)
    x1 = x_ref[...] + mods_ref[2:3, :] * mix
    x1_ref[...] = x1
    h = _rms(x1) * g_ref[...]
    h_ref[...] = (h * (1.0 + mods_ref[4:5, :]) + mods_ref[3:4, :]).astype(h_ref.dtype)


def _outproj(att, ssd, x, mods, g, w):
    b, s, d = x.shape
    tm = OUT_TM
    row = lambda bi, i: (bi, i, 0)
    return pl.pallas_call(
        _outproj_kernel,
        grid=(b, s // tm),
        in_specs=[pl.BlockSpec((None, tm, ATT_WIDTH), row),
                  pl.BlockSpec((None, tm, SSD_WIDTH), row),
                  pl.BlockSpec((None, tm, d), row),
                  pl.BlockSpec((None, N_MOD, d), lambda bi, i: (bi, 0, 0)),
                  _resident(g.shape), _resident(w.shape)],
        out_specs=[pl.BlockSpec((None, tm, d), row), pl.BlockSpec((None, tm, d), row)],
        out_shape=[jax.ShapeDtypeStruct((b, s, d), F32), jax.ShapeDtypeStruct((b, s, d), BF16)],
        compiler_params=_params("arbitrary", "arbitrary"),
        name="outproj",
    )(att, ssd, x, mods, g, w)


def _ffn_kernel(final_norm, h_ref, x1_ref, mods_ref, g_ref, wg_ref, wu_ref, wd_ref, o_ref):
    k = pl.program_id(2)

    @pl.when(k == 0)
    def _():
        o_ref[...] = x1_ref[...]

    h = h_ref[...]
    gate = jnp.dot(h, wg_ref[...], preferred_element_type=F32)
    up = jnp.dot(h, wu_ref[...], preferred_element_type=F32)
    act = (_silu(gate) * up).astype(BF16)
    d = o_ref.shape[-1]
    for n0 in range(0, d, FFN_TN):
        cols = slice(n0, n0 + FFN_TN)
        o_ref[:, cols] += mods_ref[5:6, cols] * jnp.dot(act, wd_ref[:, cols], preferred_element_type=F32)

    if final_norm:
        @pl.when(k == pl.num_programs(2) - 1)
        def _():
            o_ref[...] = _rms(o_ref[...]) * g_ref[...]


def _ffn(h, x1, mods, g_final, w_gate, w_up, w_down):
    b, s, d = x1.shape
    tm, th = FFN_TM, FFN_TH
    hid = w_gate.shape[1]
    row = lambda bi, i, k: (bi, i, 0)
    final_norm = g_final is not None
    g = g_final if final_norm else jnp.ones((1, d), F32)
    return pl.pallas_call(
        functools.partial(_ffn_kernel, final_norm),
        grid=(b, s // tm, hid // th),
        in_specs=[pl.BlockSpec((None, tm, d), row),
                  pl.BlockSpec((None, tm, d), row, pipeline_mode=pl.Buffered(1)),
                  pl.BlockSpec((None, N_MOD, d), lambda bi, i, k: (bi, 0, 0)),
                  pl.BlockSpec(g.shape, lambda bi, i, k: (0, 0)),
                  pl.BlockSpec((d, th), lambda bi, i, k: (0, k)),
                  pl.BlockSpec((d, th), lambda bi, i, k: (0, k)),
                  pl.BlockSpec((th, d), lambda bi, i, k: (k, 0))],
        out_specs=pl.BlockSpec((None, tm, d), row),
        out_shape=jax.ShapeDtypeStruct((b, s, d), F32),
        compiler_params=_params("arbitrary", "arbitrary", "arbitrary"),
        name="ffn",
    )(h, x1, mods, g, w_gate, w_up, w_down)


def _pad_lanes(v):
    return jnp.pad(v.astype(F32), (0, LANES - v.shape[0])).reshape(1, LANES)


def kernel(x, c, w_ada, b_ada, g_mix, w_in, rel_bias, conv_w, conv_b, dt_bias, a_log, d_skip,
           g_att_out, g_ssd_out, w_out, g_ffn, w_gate, w_up, w_down, g_final):
    b, s, d = x.shape
    depth = w_ada.shape[0]
    n_main = sum(_PROJ_SPLITS)
    head_of_lane = jnp.arange(SSD_WIDTH) // SSD_HEAD_DIM
    expand = (jnp.arange(2 * LANES)[:, None] % LANES == head_of_lane[None, :]).astype(BF16)
    c_pad = jnp.pad(c, ((0, 8 - b % 8 if b % 8 else 0), (0, 0)))
    for l in range(depth):
        mods = _adaln(c_pad, w_ada[l], b_ada[l].reshape(1, -1))[:b].reshape(b, N_MOD, d)
        n_in = w_in.shape[-1]
        q_scale = jnp.where(jnp.arange(n_in) < ATT_WIDTH, ATT_HEAD_DIM ** -0.5 * LOG2E, 1.0).astype(F32)
        w_proj = (w_in[l] * q_scale).astype(BF16)
        w_dt = jnp.pad(w_in[l][:, n_main:], ((0, 0), (0, LANES - (n_in - n_main)))).astype(BF16)
        q, k, v, z, xbc, dt_raw = _inproj(x, mods, g_mix[l].reshape(1, d), w_proj, w_dt)
        att = _attention(q, k, v, _bias_table(rel_bias[l]), g_att_out[l].reshape(1, -1))
        ssd = _ssd(xbc, z, dt_raw, conv_w[l], conv_b[l].reshape(1, -1), _pad_lanes(dt_bias[l]),
                   _pad_lanes(a_log[l]), d_skip[l].astype(F32)[head_of_lane].reshape(1, -1), expand,
                   g_ssd_out[l].reshape(1, -1))
        x1, h2 = _outproj(att, ssd, x, mods, g_ffn[l].reshape(1, d), w_out[l].astype(BF16))
        g_last = g_final.reshape(1, d) if l == depth - 1 else None
        x = _ffn(h2, x1, mods, g_last, w_gate[l].astype(BF16), w_up[l].astype(BF16),
                 w_down[l].astype(BF16))
    return x
```

```python
import functools

import jax
import jax.numpy as jnp
from jax import lax
from jax.experimental import pallas as pl
from jax.experimental.pallas import tpu as pltpu

F32 = jnp.float32
BF16 = jnp.bfloat16

D_MODEL = 2048
CHUNK = 64
LEFT_CHUNKS = 8
BAND = (LEFT_CHUNKS + 1) * CHUNK
ATT_HEADS = 16
ATT_HEAD_DIM = 64
ATT_WIDTH = ATT_HEADS * ATT_HEAD_DIM
REL_CLIP = 256
REL_FUTURE = CHUNK - 1
SSD_HEADS = 16
SSD_HEAD_DIM = 64
SSD_WIDTH = SSD_HEADS * SSD_HEAD_DIM
SSD_GROUPS = 2
SSD_STATE = 128
SSD_CONV = 4
XBC_WIDTH = SSD_WIDTH + 2 * SSD_GROUPS * SSD_STATE
GROUP_WIDTH = SSD_WIDTH // SSD_GROUPS
FFN_HIDDEN = 5632
N_MOD = 6
EPS = 1e-6

LANES = 128
MXU_WIDTH = 256
ATT_GROUP_WIDTH = MXU_WIDTH
ATT_GROUP_HEADS = ATT_GROUP_WIDTH // ATT_HEAD_DIM
ATT_GROUPS = ATT_WIDTH // ATT_GROUP_WIDTH
BAND_PAD = -(-BAND // LANES) * LANES
VMEM_LIMIT = 60 * 1024 * 1024

ADA_TN = 1024
PROJ_TM = 512
PROJ_TN = 512
ATT_TQ = 512
SSD_TB = 512
SSD_L = 128
OUT_TM = 512
OUT_SUB = 256
FFN_TM = 1024
FFN_TH = 512
FFN_TN = 512

HIGHEST = lax.Precision.HIGHEST


def _params(*sem):
    return pltpu.CompilerParams(dimension_semantics=sem, vmem_limit_bytes=VMEM_LIMIT)


def _resident(shape):
    nd = len(shape)
    return pl.BlockSpec(shape, lambda *_: (0,) * nd, pipeline_mode=pl.Buffered(1))


def _rms(x):
    return x * lax.rsqrt(jnp.mean(x * x, axis=-1, keepdims=True) + EPS)


def _silu(x):
    h = 0.5 * x
    return h + h * jnp.tanh(h)


def _split_bf16(x, terms):
    parts = []
    for _ in range(terms):
        p = x.astype(BF16)
        parts.append(p)
        x = x - p.astype(F32)
    return parts


def _adaln_kernel(c_ref, w_ref, b_ref, o_ref):
    cond = _silu(c_ref[...])
    o_ref[...] = jnp.dot(cond.astype(BF16), w_ref[...].astype(BF16),
                         preferred_element_type=F32) + b_ref[...]


def _adaln(c_pad, w, b):
    rows, d = c_pad.shape
    n = w.shape[1]
    return pl.pallas_call(
        _adaln_kernel,
        grid=(n // ADA_TN,),
        in_specs=[pl.BlockSpec((rows, d), lambda j: (0, 0)),
                  pl.BlockSpec((d, ADA_TN), lambda j: (0, j)),
                  pl.BlockSpec((1, ADA_TN), lambda j: (0, j))],
        out_specs=pl.BlockSpec((rows, ADA_TN), lambda j: (0, j)),
        out_shape=jax.ShapeDtypeStruct((rows, n), F32),
        compiler_params=_params("arbitrary"),
        name="adaln",
    )(c_pad, w, b)


_PROJ_SPLITS = (ATT_WIDTH, ATT_WIDTH, ATT_WIDTH, SSD_WIDTH, XBC_WIDTH)
_NT = (((1,), (1,)), ((), ()))


def _inproj_kernel(x_ref, mods_ref, g_ref, w_ref, wdt_ref,
                   q_ref, k_ref, v_ref, z_ref, xbc_ref, dt_ref, h_s):
    x = x_ref[...]
    h = _rms(x) * g_ref[...]
    h = h * (1.0 + mods_ref[1:2, :]) + mods_ref[0:1, :]
    h_s[...] = h.astype(BF16)
    col = 0
    for ref, width in zip((q_ref, k_ref, v_ref, z_ref, xbc_ref), _PROJ_SPLITS):
        for j in range(0, width, PROJ_TN):
            ref[:, j:j + PROJ_TN] = lax.dot_general(
                h_s[...], w_ref[col + j:col + j + PROJ_TN, :], _NT,
                preferred_element_type=F32).astype(ref.dtype)
        col += width
    dt_ref[...] = lax.dot_general(h_s[...], wdt_ref[...], _NT, preferred_element_type=F32)


def _inproj(x, mods, g, w, w_dt):
    b, s, d = x.shape
    tm = PROJ_TM
    row = lambda bi, i: (bi, i, 0)
    out_shape = [jax.ShapeDtypeStruct((b, s, w), BF16) for w in _PROJ_SPLITS]
    out_shape.append(jax.ShapeDtypeStruct((b, s, LANES), F32))
    out_specs = [pl.BlockSpec((None, tm, w), row) for w in _PROJ_SPLITS]
    out_specs.append(pl.BlockSpec((None, tm, LANES), row))
    return pl.pallas_call(
        _inproj_kernel,
        grid=(b, s // tm),
        in_specs=[pl.BlockSpec((None, tm, d), row),
                  pl.BlockSpec((None, N_MOD, d), lambda bi, i: (bi, 0, 0)),
                  _resident(g.shape), _resident(w.shape), _resident(w_dt.shape)],
        out_specs=out_specs,
        out_shape=out_shape,
        scratch_shapes=[pltpu.VMEM((tm, d), BF16)],
        compiler_params=_params("arbitrary", "arbitrary"),
        name="inproj",
    )(x, mods, g, w, w_dt)


LOG2E = 1.4426950408889634


def _attn_kernel(q_ref, kp_ref, kc_ref, vp_ref, vc_ref, bias_ref, g_ref, o_ref,
                 k_s, v_s, s_s, pt_s, max_s, den_s, o_s, inv_s):
    i = pl.program_id(1)
    tq = ATT_TQ
    k_s[0:tq, :] = kp_ref[...]
    k_s[tq:, :] = kc_ref[...]
    v_s[0:tq, :] = vp_ref[...]
    v_s[tq:2 * tq, :] = vc_ref[...]
    v_s[2 * tq:, :] = jnp.zeros((BAND_PAD - BAND, ATT_WIDTH), BF16)

    gw = ATT_GROUP_WIDTH
    rr = lax.broadcasted_iota(jnp.int32, (gw, gw), 0) // CHUNK
    ll = lax.broadcasted_iota(jnp.int32, (gw, gw), 1) // ATT_HEAD_DIM
    own_head = rr == ll
    lane_head = lax.broadcasted_iota(jnp.int32, (CHUNK, gw), 1) // ATT_HEAD_DIM
    groups = range(ATT_GROUPS)
    n_chunks = tq // CHUNK

    def scores_stage(c, slot):
        q0 = c * CHUNK if isinstance(c, int) else pl.multiple_of(c * CHUNK, CHUNK)
        for g in groups:
            lanes = slice(g * gw, (g + 1) * gw)
            qg = q_ref[pl.ds(q0, CHUNK), lanes]
            q_bd = jnp.where(own_head, jnp.concatenate([qg] * ATT_GROUP_HEADS, axis=0), jnp.zeros((), BF16))
            kb = k_s[pl.ds(q0, BAND), lanes]
            st = lax.dot_general(kb, q_bd, (((1,), (1,)), ((), ())),
                                 preferred_element_type=F32) + bias_ref[g]
            s_s[slot, g] = st
            max_s[slot, g] = jnp.broadcast_to(jnp.max(st, axis=0, keepdims=True), (8, gw))

    def mask_stage(c, slot):
        n_before = jnp.maximum(LEFT_CHUNKS - (i * n_chunks + c), 0)

        def mask_slab(j, carry):
            r0 = pl.multiple_of(j * CHUNK, CHUNK)
            for g in groups:
                s_s[slot, g, pl.ds(r0, CHUNK), :] = jnp.full((CHUNK, gw), -jnp.inf, F32)
            return carry

        lax.fori_loop(0, n_before, mask_slab, 0)

        @pl.when(n_before > 0)
        def _():
            for g in groups:
                max_s[slot, g] = jnp.broadcast_to(jnp.max(s_s[slot, g], axis=0, keepdims=True), (8, gw))

    def softmax_stage(slot):
        for g in groups:
            e = jnp.exp2(s_s[slot, g] - max_s[slot, g, 0:1, :])
            den_s[slot, g] = jnp.broadcast_to(jnp.sum(e, axis=0, keepdims=True), (8, gw))
            eb = jnp.concatenate([e.astype(BF16), jnp.zeros((BAND_PAD - BAND, gw), BF16)], axis=0)
            for r in range(0, BAND_PAD, LANES):
                pt_s[slot, g, :, r:r + LANES] = eb[r:r + LANES, :].T

    def pv_stage(c, slot):
        q0 = c * CHUNK if isinstance(c, int) else pl.multiple_of(c * CHUNK, CHUNK)
        for g in groups:
            lanes = slice(g * gw, (g + 1) * gw)
            vb = v_s[pl.ds(q0, BAND_PAD), lanes]
            o4 = jnp.dot(pt_s[slot, g], vb, preferred_element_type=F32)
            out = o4[0:CHUNK]
            for hj in range(1, ATT_GROUP_HEADS):
                out = jnp.where(lane_head == hj, o4[hj * CHUNK:(hj + 1) * CHUNK], out)
            o_s[slot, :, lanes] = out
            inv_s[slot, g] = 1.0 / den_s[slot, g]

    def finalize_stage(c, slot):
        q0 = c * CHUNK if isinstance(c, int) else pl.multiple_of(c * CHUNK, CHUNK)
        scales = []
        for g in groups:
            inv_den = inv_s[slot, g, 0:1, :]
            scale = None
            for half in range(gw // LANES):
                col = jnp.broadcast_to(inv_den[:, half * LANES:(half + 1) * LANES], (LANES, LANES)).T
                for j in range(LANES // CHUNK):
                    hj = half * (LANES // CHUNK) + j
                    blk = jnp.concatenate([col[j * CHUNK:(j + 1) * CHUNK]] * (gw // LANES), axis=1)
                    scale = blk if scale is None else jnp.where(lane_head == hj, blk, scale)
            scales.append(scale)
        att = o_s[slot] * jnp.concatenate(scales, axis=-1)
        o_ref[pl.ds(q0, CHUNK), :] = (_rms(att) * g_ref[...]).astype(o_ref.dtype)

    def block(c, slot, first=0, last=n_chunks - 1):
        if first <= c <= last:
            scores_stage(c, slot)
        if first <= c - 1 <= last:
            softmax_stage(1 - slot)
        if first <= c - 2 <= last:
            pv_stage(c - 2, slot)
        if first <= c - 3 <= last:
            finalize_stage(c - 3, 1 - slot)
        if first <= c <= last:
            mask_stage(c, slot)

    def steady_pair(j, carry):
        c = 3 + 2 * j
        for k in range(2):
            scores_stage(c + k, (1 + k) % 2)
            softmax_stage(k % 2)
            pv_stage(c + k - 2, (1 + k) % 2)
            finalize_stage(c + k - 3, k % 2)
            mask_stage(c + k, (1 + k) % 2)
        return carry

    n_pairs = (n_chunks - 4) // 2
    assert n_chunks >= 4 and n_chunks % 2 == 0
    for c in range(3):
        block(c, c % 2)
    lax.fori_loop(0, n_pairs, steady_pair, 0)
    for c in range(3 + 2 * n_pairs, n_chunks + 3):
        block(c, c % 2)


def _attention(q, k, v, bias_t, g):
    b, s, w = q.shape
    tq = ATT_TQ
    assert tq == LEFT_CHUNKS * CHUNK
    cur = lambda bi, i: (bi, i, 0)
    prev = lambda bi, i: (bi, jnp.maximum(i - 1, 0), 0)
    blk = (None, tq, w)
    return pl.pallas_call(
        _attn_kernel,
        grid=(b, s // tq),
        in_specs=[pl.BlockSpec(blk, cur),
                  pl.BlockSpec(blk, prev), pl.BlockSpec(blk, cur),
                  pl.BlockSpec(blk, prev), pl.BlockSpec(blk, cur),
                  _resident(bias_t.shape), _resident(g.shape)],
        out_specs=pl.BlockSpec(blk, cur),
        out_shape=jax.ShapeDtypeStruct((b, s, w), BF16),
        scratch_shapes=[pltpu.VMEM((2 * tq, w), BF16),
                        pltpu.VMEM((2 * tq + BAND_PAD - BAND, w), BF16),
                        pltpu.VMEM((2, ATT_GROUPS, BAND, ATT_GROUP_WIDTH), F32),
                        pltpu.VMEM((2, ATT_GROUPS, ATT_GROUP_WIDTH, BAND_PAD), BF16),
                        pltpu.VMEM((2, ATT_GROUPS, 8, ATT_GROUP_WIDTH), F32),
                        pltpu.VMEM((2, ATT_GROUPS, 8, ATT_GROUP_WIDTH), F32),
                        pltpu.VMEM((2, CHUNK, w), F32),
                        pltpu.VMEM((2, ATT_GROUPS, 8, ATT_GROUP_WIDTH), F32)],
        compiler_params=_params("arbitrary", "arbitrary"),
        name="chunk_attn",
    )(q, k, k, v, v, bias_t, g)


def _bias_table(rel_bias):
    n_rel = rel_bias.shape[1]
    n_f = BAND + REL_FUTURE
    f = jnp.concatenate([jnp.broadcast_to(rel_bias[:, n_rel - 1:], (ATT_HEADS, n_f - n_rel)),
                         rel_bias[:, ::-1]], axis=1).astype(F32) * LOG2E
    skew = jnp.broadcast_to(jnp.pad(f, ((0, 0), (0, 1)))[:, None, :], (ATT_HEADS, CHUNK, n_f + 1))
    skew = skew.reshape(ATT_HEADS, CHUNK * (n_f + 1))[:, :CHUNK * n_f].reshape(ATT_HEADS, CHUNK, n_f)
    bias = skew[:, :, REL_FUTURE:]
    bias = bias.reshape(ATT_GROUPS, ATT_GROUP_HEADS, CHUNK, BAND)
    return jnp.transpose(bias, (0, 3, 1, 2)).reshape(ATT_GROUPS, BAND, ATT_GROUP_WIDTH)


_HALO = 8


def _ssd_kernel(xbc_ref, z_ref, dt_ref, convw_ref, convb_ref, dtb_ref, a_ref, dskip_ref,
                expand_ref, g_ref, o_ref, raw_s, act_s, dt_s, y_s, h_s):
    i = pl.program_id(1)
    tb = SSD_TB
    L = SSD_L

    @pl.when(i == 0)
    def _():
        raw_s[0:_HALO, :] = jnp.zeros((_HALO, XBC_WIDTH), F32)
        h_s[...] = jnp.zeros_like(h_s)

    @pl.when(i > 0)
    def _():
        raw_s[0:_HALO, :] = raw_s[tb:tb + _HALO, :]

    raw_s[_HALO:, :] = xbc_ref[...].astype(F32)
    assert SSD_CONV == 4
    raw = raw_s[...]
    back1 = pltpu.roll(raw, 1, axis=0)
    near = convw_ref[3:4, :] * raw + convw_ref[2:3, :] * back1
    far = convw_ref[1:2, :] * raw + convw_ref[0:1, :] * back1
    conv = convb_ref[...] + near[_HALO:] + pltpu.roll(far, 2, axis=0)[_HALO:]
    act_s[...] = _silu(conv)

    dtr = dt_ref[...] + dtb_ref[...]
    dt_s[...] = jnp.maximum(dtr, 0.0) + jnp.log1p(jnp.exp(-jnp.abs(dtr)))

    a = -jnp.exp(a_ref[...])
    ri = lax.broadcasted_iota(jnp.int32, (L, L), 0)
    ci = lax.broadcasted_iota(jnp.int32, (L, L), 1)
    causal = ri >= ci
    tril = causal.astype(BF16)
    low_lanes = lax.broadcasted_iota(jnp.int32, (L, LANES), 1) < SSD_HEAD_DIM
    expand = expand_ref[...]
    dskip = dskip_ref[...]
    bc0 = SSD_WIDTH
    cc0 = SSD_WIDTH + SSD_GROUPS * SSD_STATE

    def chunk_body(c, carry):
        r0 = pl.multiple_of(c * L, L)
        dtc = dt_s[pl.ds(r0, L), :]
        adt = dtc * a
        cs3 = jnp.dot(tril, jnp.concatenate(_split_bf16(adt, 3), axis=1), preferred_element_type=F32)
        cs = cs3[:, 0:LANES] + cs3[:, LANES:2 * LANES] + cs3[:, 2 * LANES:]
        cs_t = cs.T
        last = cs[L - 1:L, :]
        stacked = jnp.concatenate([dtc, jnp.exp(last - cs), jnp.exp(cs)], axis=0)
        wide = jnp.dot(jnp.concatenate(_split_bf16(stacked, 2), axis=1), expand,
                       preferred_element_type=F32)
        dt_w, decay_w, ecs_w = wide[0:L], wide[L:2 * L], wide[2 * L:3 * L]
        cdec = jnp.broadcast_to(jnp.exp(last), (8, LANES))
        cdec_w = sum(jnp.dot(part, expand[:LANES], preferred_element_type=F32)
                     for part in _split_bf16(cdec, 3))[0:1]

        xs = act_s[pl.ds(r0, L), 0:SSD_WIDTH]
        xdt = xs * dt_w
        xdt_b = xdt.astype(BF16)
        xdec_b = (xdt * decay_w).astype(BF16)
        ys = []
        for g in range(SSD_GROUPS):
            gl = slice(g * GROUP_WIDTH, (g + 1) * GROUP_WIDTH)
            bm = act_s[pl.ds(r0, L), bc0 + g * SSD_STATE:bc0 + (g + 1) * SSD_STATE].astype(BF16)
            cm = act_s[pl.ds(r0, L), cc0 + g * SSD_STATE:cc0 + (g + 1) * SSD_STATE].astype(BF16)
            cb = lax.dot_general(cm, bm, (((1,), (1,)), ((), ())), preferred_element_type=F32)
            hprev = h_s[g]
            y_off = jnp.dot(cm, hprev.astype(BF16), preferred_element_type=F32) * ecs_w[:, gl]
            states = lax.dot_general(bm, xdec_b[:, gl], (((0,), (0,)), ((), ())),
                                     preferred_element_type=F32)
            h_s[g] = hprev * cdec_w[:, gl] + states
            yd = []
            for pr in range(GROUP_WIDTH // LANES):
                hp = g * (GROUP_WIDTH // LANES) + pr
                xp = xdt_b[:, hp * LANES:(hp + 1) * LANES]
                halves = []
                for hh in (2 * hp, 2 * hp + 1):
                    seg = jnp.exp(jnp.where(causal, cs[:, hh:hh + 1] - cs_t[hh:hh + 1, :], -jnp.inf))
                    halves.append(jnp.dot((cb * seg).astype(BF16), xp, preferred_element_type=F32))
                yd.append(jnp.where(low_lanes, halves[0], halves[1]))
            ys.append(jnp.concatenate(yd, axis=-1) + y_off)
        y = jnp.concatenate(ys, axis=-1) + xs * dskip
        y_s[pl.ds(r0, L), :] = y
        return carry

    lax.fori_loop(0, tb // L, chunk_body, 0)
    gated = y_s[...] * _silu(z_ref[...].astype(F32))
    o_ref[...] = (_rms(gated) * g_ref[...]).astype(o_ref.dtype)


def _ssd(xbc, z, dt_raw, conv_w, conv_b, dt_bias, a_log, dskip_w, expand, g):
    b, s, _ = xbc.shape
    tb = SSD_TB
    row = lambda bi, i: (bi, i, 0)
    return pl.pallas_call(
        _ssd_kernel,
        grid=(b, s // tb),
        in_specs=[pl.BlockSpec((None, tb, XBC_WIDTH), row),
                  pl.BlockSpec((None, tb, SSD_WIDTH), row),
                  pl.BlockSpec((None, tb, LANES), row),
                  _resident(conv_w.shape), _resident(conv_b.shape), _resident(dt_bias.shape),
                  _resident(a_log.shape), _resident(dskip_w.shape), _resident(expand.shape),
                  _resident(g.shape)],
        out_specs=pl.BlockSpec((None, tb, SSD_WIDTH), row),
        out_shape=jax.ShapeDtypeStruct((b, s, SSD_WIDTH), BF16),
        scratch_shapes=[pltpu.VMEM((_HALO + tb, XBC_WIDTH), F32),
                        pltpu.VMEM((tb, XBC_WIDTH), F32),
                        pltpu.VMEM((tb, LANES), F32),
                        pltpu.VMEM((tb, SSD_WIDTH), F32),
                        pltpu.VMEM((SSD_GROUPS, SSD_STATE, GROUP_WIDTH), F32)],
        compiler_params=_params("arbitrary", "arbitrary"),
        name="ssd",
    )(xbc, z, dt_raw, conv_w, conv_b, dt_bias, a_log, dskip_w, expand, g)


def _outproj_kernel(att_ref, ssd_ref, x_ref, mods_ref, g_ref, w_ref, x1_ref, h_ref):
    for r0 in range(0, x_ref.shape[0], OUT_SUB):
        rows = slice(r0, r0 + OUT_SUB)
        mix = jnp.dot(att_ref[rows, :], w_ref[0:ATT_WIDTH, :], preferred_element_type=F32)
        mix = mix + jnp.dot(ssd_ref[rows, :], w_ref[ATT_WIDTH:, :], preferred_element_type=F32)
        x1 = x_ref[rows, :] + mods_ref[2:3, :] * mix
        x1_ref[rows, :] = x1
        h = _rms(x1) * g_ref[...]
        h_ref[rows, :] = (h * (1.0 + mods_ref[4:5, :]) + mods_ref[3:4, :]).astype(h_ref.dtype)


def _outproj(att, ssd, x, mods, g, w):
    b, s, d = x.shape
    tm = OUT_TM
    row = lambda bi, i: (bi, i, 0)
    return pl.pallas_call(
        _outproj_kernel,
        grid=(b, s // tm),
        in_specs=[pl.BlockSpec((None, tm, ATT_WIDTH), row),
                  pl.BlockSpec((None, tm, SSD_WIDTH), row),
                  pl.BlockSpec((None, tm, d), row),
                  pl.BlockSpec((None, N_MOD, d), lambda bi, i: (bi, 0, 0)),
                  _resident(g.shape), _resident(w.shape)],
        out_specs=[pl.BlockSpec((None, tm, d), row), pl.BlockSpec((None, tm, d), row)],
        out_shape=[jax.ShapeDtypeStruct((b, s, d), F32), jax.ShapeDtypeStruct((b, s, d), BF16)],
        compiler_params=_params("arbitrary", "arbitrary"),
        name="outproj",
    )(att, ssd, x, mods, g, w)


def _ffn_kernel(final_norm, h_ref, x1_ref, mods_ref, g_ref, wg_ref, wu_ref, wd_ref, o_ref):
    k = pl.program_id(2)

    @pl.when(k == 0)
    def _():
        o_ref[...] = x1_ref[...]

    h = h_ref[...]
    gate = jnp.dot(h, wg_ref[...], preferred_element_type=F32)
    up = jnp.dot(h, wu_ref[...], preferred_element_type=F32)
    act = (_silu(gate) * up).astype(BF16)
    d = o_ref.shape[-1]
    for n0 in range(0, d, FFN_TN):
        cols = slice(n0, n0 + FFN_TN)
        o_ref[:, cols] += mods_ref[5:6, cols] * jnp.dot(act, wd_ref[:, cols], preferred_element_type=F32)

    if final_norm:
        @pl.when(k == pl.num_programs(2) - 1)
        def _():
            o_ref[...] = _rms(o_ref[...]) * g_ref[...]


def _ffn(h, x1, mods, g_final, w_gate, w_up, w_down):
    b, s, d = x1.shape
    tm, th = FFN_TM, FFN_TH
    hid = w_gate.shape[1]
    row = lambda bi, i, k: (bi, i, 0)
    final_norm = g_final is not None
    g = g_final if final_norm else jnp.ones((1, d), F32)
    return pl.pallas_call(
        functools.partial(_ffn_kernel, final_norm),
        grid=(b, s // tm, hid // th),
        in_specs=[pl.BlockSpec((None, tm, d), row),
                  pl.BlockSpec((None, tm, d), row),
                  pl.BlockSpec((None, N_MOD, d), lambda bi, i, k: (bi, 0, 0)),
                  pl.BlockSpec(g.shape, lambda bi, i, k: (0, 0)),
                  pl.BlockSpec((d, th), lambda bi, i, k: (0, k)),
                  pl.BlockSpec((d, th), lambda bi, i, k: (0, k)),
                  pl.BlockSpec((th, d), lambda bi, i, k: (k, 0))],
        out_specs=pl.BlockSpec((None, tm, d), row),
        out_shape=jax.ShapeDtypeStruct((b, s, d), F32),
        compiler_params=_params("arbitrary", "arbitrary", "arbitrary"),
        name="ffn",
    )(h, x1, mods, g, w_gate, w_up, w_down)


def _pad_lanes(v):
    return jnp.pad(v.astype(F32), (0, LANES - v.shape[0])).reshape(1, LANES)


def kernel(x, c, w_ada, b_ada, g_mix, w_in, rel_bias, conv_w, conv_b, dt_bias, a_log, d_skip,
           g_att_out, g_ssd_out, w_out, g_ffn, w_gate, w_up, w_down, g_final):
    b, s, d = x.shape
    depth = w_ada.shape[0]
    n_main = sum(_PROJ_SPLITS)
    head_of_lane = jnp.arange(SSD_WIDTH) // SSD_HEAD_DIM
    expand = (jnp.arange(2 * LANES)[:, None] % LANES == head_of_lane[None, :]).astype(BF16)
    c_pad = jnp.pad(c, ((0, 8 - b % 8 if b % 8 else 0), (0, 0)))
    for l in range(depth):
        mods = _adaln(c_pad, w_ada[l], b_ada[l].reshape(1, -1))[:b].reshape(b, N_MOD, d)
        n_in = w_in.shape[-1]
        q_scale = jnp.where(jnp.arange(n_in) < ATT_WIDTH, ATT_HEAD_DIM ** -0.5 * LOG2E, 1.0).astype(F32)
        w_t = w_in[l].T
        w_proj = (w_t * q_scale[:, None]).astype(BF16)
        w_dt = jnp.pad(w_t[n_main:], ((0, LANES - (n_in - n_main)), (0, 0))).astype(BF16)
        q, k, v, z, xbc, dt_raw = _inproj(x, mods, g_mix[l].reshape(1, d), w_proj, w_dt)
        att = _attention(q, k, v, _bias_table(rel_bias[l]), g_att_out[l].reshape(1, -1))
        ssd = _ssd(xbc, z, dt_raw, conv_w[l], conv_b[l].reshape(1, -1), _pad_lanes(dt_bias[l]),
                   _pad_lanes(a_log[l]), d_skip[l].astype(F32)[head_of_lane].reshape(1, -1), expand,
                   g_ssd_out[l].reshape(1, -1))
        x1, h2 = _outproj(att, ssd, x, mods, g_ffn[l].reshape(1, d), w_out[l].astype(BF16))
        g_last = g_final.reshape(1, d) if l == depth - 1 else None
        x = _ffn(h2, x1, mods, g_last, w_gate[l].astype(BF16), w_up[l].astype(BF16),
                 w_down[l].astype(BF16))
    return x
```

```python
import functools

import jax
import jax.numpy as jnp
from jax import lax
from jax.experimental import pallas as pl
from jax.experimental.pallas import tpu as pltpu

F32 = jnp.float32
BF16 = jnp.bfloat16

D_MODEL = 2048
CHUNK = 64
LEFT_CHUNKS = 8
BAND = (LEFT_CHUNKS + 1) * CHUNK
ATT_HEADS = 16
ATT_HEAD_DIM = 64
ATT_WIDTH = ATT_HEADS * ATT_HEAD_DIM
REL_CLIP = 256
REL_FUTURE = CHUNK - 1
SSD_HEADS = 16
SSD_HEAD_DIM = 64
SSD_WIDTH = SSD_HEADS * SSD_HEAD_DIM
SSD_GROUPS = 2
SSD_STATE = 128
SSD_CONV = 4
XBC_WIDTH = SSD_WIDTH + 2 * SSD_GROUPS * SSD_STATE
GROUP_WIDTH = SSD_WIDTH // SSD_GROUPS
FFN_HIDDEN = 5632
N_MOD = 6
EPS = 1e-6

LANES = 128
MXU_WIDTH = 256
ATT_GROUP_WIDTH = MXU_WIDTH
ATT_GROUP_HEADS = ATT_GROUP_WIDTH // ATT_HEAD_DIM
ATT_GROUPS = ATT_WIDTH // ATT_GROUP_WIDTH
BAND_PAD = -(-BAND // LANES) * LANES
VMEM_LIMIT = 60 * 1024 * 1024

ADA_TN = 1024
PROJ_TM = 512
PROJ_TN = 512
ATT_TQ = 1024
ATT_LEFT = LEFT_CHUNKS * CHUNK
SSD_TB = 512
SSD_L = 256
OUT_TM = 512
OUT_SUB = 256
FFN_TM = 1024
FFN_TH = 512
FFN_TN = 512

HIGHEST = lax.Precision.HIGHEST


def _params(*sem):
    return pltpu.CompilerParams(dimension_semantics=sem, vmem_limit_bytes=VMEM_LIMIT)


def _resident(shape):
    nd = len(shape)
    return pl.BlockSpec(shape, lambda *_: (0,) * nd, pipeline_mode=pl.Buffered(1))


def _rms(x):
    return x * lax.rsqrt(jnp.mean(x * x, axis=-1, keepdims=True) + EPS)


def _silu(x):
    h = 0.5 * x
    return h + h * jnp.tanh(h)


def _split_bf16(x, terms):
    parts = []
    for _ in range(terms):
        p = x.astype(BF16)
        parts.append(p)
        x = x - p.astype(F32)
    return parts


def _adaln_kernel(c_ref, w_ref, b_ref, o_ref):
    cond = _silu(c_ref[...])
    o_ref[...] = jnp.dot(cond.astype(BF16), w_ref[...].astype(BF16),
                         preferred_element_type=F32) + b_ref[...]


def _adaln(c_pad, w, b):
    rows, d = c_pad.shape
    n = w.shape[1]
    return pl.pallas_call(
        _adaln_kernel,
        grid=(n // ADA_TN,),
        in_specs=[pl.BlockSpec((rows, d), lambda j: (0, 0)),
                  pl.BlockSpec((d, ADA_TN), lambda j: (0, j)),
                  pl.BlockSpec((1, ADA_TN), lambda j: (0, j))],
        out_specs=pl.BlockSpec((rows, ADA_TN), lambda j: (0, j)),
        out_shape=jax.ShapeDtypeStruct((rows, n), F32),
        compiler_params=_params("arbitrary"),
        name="adaln",
    )(c_pad, w, b)


_PROJ_SPLITS = (ATT_WIDTH, ATT_WIDTH, ATT_WIDTH, SSD_WIDTH, XBC_WIDTH)
_NT = (((1,), (1,)), ((), ()))


def _inproj_kernel(x_ref, mods_ref, g_ref, w_ref, wdt_ref,
                   q_ref, k_ref, v_ref, z_ref, xbc_ref, dt_ref, h_s):
    x = x_ref[...]
    h = _rms(x) * g_ref[...]
    h = h * (1.0 + mods_ref[1:2, :]) + mods_ref[0:1, :]
    h_s[...] = h.astype(BF16)
    col = 0
    for ref, width in zip((q_ref, k_ref, v_ref, z_ref, xbc_ref), _PROJ_SPLITS):
        for j in range(0, width, PROJ_TN):
            ref[:, j:j + PROJ_TN] = lax.dot_general(
                h_s[...], w_ref[col + j:col + j + PROJ_TN, :], _NT,
                preferred_element_type=F32).astype(ref.dtype)
        col += width
    dt_ref[...] = lax.dot_general(h_s[...], wdt_ref[...], _NT, preferred_element_type=F32)


def _inproj(x, mods, g, w, w_dt):
    b, s, d = x.shape
    tm = PROJ_TM
    row = lambda bi, i: (bi, i, 0)
    out_shape = [jax.ShapeDtypeStruct((b, s, w), BF16) for w in _PROJ_SPLITS]
    out_shape.append(jax.ShapeDtypeStruct((b, s, LANES), F32))
    out_specs = [pl.BlockSpec((None, tm, w), row) for w in _PROJ_SPLITS]
    out_specs.append(pl.BlockSpec((None, tm, LANES), row))
    return pl.pallas_call(
        _inproj_kernel,
        grid=(b, s // tm),
        in_specs=[pl.BlockSpec((None, tm, d), row),
                  pl.BlockSpec((None, N_MOD, d), lambda bi, i: (bi, 0, 0)),
                  _resident(g.shape), _resident(w.shape), _resident(w_dt.shape)],
        out_specs=out_specs,
        out_shape=out_shape,
        scratch_shapes=[pltpu.VMEM((tm, d), BF16)],
        compiler_params=_params("arbitrary", "arbitrary"),
        name="inproj",
    )(x, mods, g, w, w_dt)


LOG2E = 1.4426950408889634


def _attn_kernel(q_ref, kp_ref, kc_ref, vp_ref, vc_ref, bias_ref, g_ref, o_ref,
                 k_s, v_s, s_s, pt_s, max_s, den_s, o_s, inv_s):
    i = pl.program_id(1)
    tq = ATT_TQ
    k_s[0:ATT_LEFT, :] = kp_ref[...]
    k_s[ATT_LEFT:, :] = kc_ref[...]
    v_s[0:ATT_LEFT, :] = vp_ref[...]
    v_s[ATT_LEFT:ATT_LEFT + tq, :] = vc_ref[...]
    v_s[ATT_LEFT + tq:, :] = jnp.zeros((BAND_PAD - BAND, ATT_WIDTH), BF16)

    gw = ATT_GROUP_WIDTH
    rr = lax.broadcasted_iota(jnp.int32, (gw, gw), 0) // CHUNK
    ll = lax.broadcasted_iota(jnp.int32, (gw, gw), 1) // ATT_HEAD_DIM
    own_head = rr == ll
    lane_head = lax.broadcasted_iota(jnp.int32, (CHUNK, gw), 1) // ATT_HEAD_DIM
    groups = range(ATT_GROUPS)
    n_chunks = tq // CHUNK

    def scores_stage(c, slot):
        q0 = c * CHUNK if isinstance(c, int) else pl.multiple_of(c * CHUNK, CHUNK)
        for g in groups:
            lanes = slice(g * gw, (g + 1) * gw)
            qg = q_ref[pl.ds(q0, CHUNK), lanes]
            q_bd = jnp.where(own_head, jnp.concatenate([qg] * ATT_GROUP_HEADS, axis=0), jnp.zeros((), BF16))
            kb = k_s[pl.ds(q0, BAND), lanes]
            st = lax.dot_general(kb, q_bd, (((1,), (1,)), ((), ())),
                                 preferred_element_type=F32) + bias_ref[g]
            s_s[slot, g] = st
            max_s[slot, g] = jnp.broadcast_to(jnp.max(st, axis=0, keepdims=True), (8, gw))

    def mask_stage(c, slot):
        n_before = jnp.maximum(LEFT_CHUNKS - (i * n_chunks + c), 0)

        def mask_slab(j, carry):
            r0 = pl.multiple_of(j * CHUNK, CHUNK)
            for g in groups:
                s_s[slot, g, pl.ds(r0, CHUNK), :] = jnp.full((CHUNK, gw), -jnp.inf, F32)
            return carry

        lax.fori_loop(0, n_before, mask_slab, 0)

        @pl.when(n_before > 0)
        def _():
            for g in groups:
                max_s[slot, g] = jnp.broadcast_to(jnp.max(s_s[slot, g], axis=0, keepdims=True), (8, gw))

    def softmax_stage(slot):
        for g in groups:
            e = jnp.exp2(s_s[slot, g] - max_s[slot, g, 0:1, :])
            den_s[slot, g] = jnp.broadcast_to(jnp.sum(e, axis=0, keepdims=True), (8, gw))
            eb = jnp.concatenate([e.astype(BF16), jnp.zeros((BAND_PAD - BAND, gw), BF16)], axis=0)
            for r in range(0, BAND_PAD, LANES):
                pt_s[slot, g, :, r:r + LANES] = eb[r:r + LANES, :].T

    def pv_stage(c, slot):
        q0 = c * CHUNK if isinstance(c, int) else pl.multiple_of(c * CHUNK, CHUNK)
        for g in groups:
            lanes = slice(g * gw, (g + 1) * gw)
            vb = v_s[pl.ds(q0, BAND_PAD), lanes]
            o4 = jnp.dot(pt_s[slot, g], vb, preferred_element_type=F32)
            out = o4[0:CHUNK]
            for hj in range(1, ATT_GROUP_HEADS):
                out = jnp.where(lane_head == hj, o4[hj * CHUNK:(hj + 1) * CHUNK], out)
            o_s[slot, :, lanes] = out
            inv_s[slot, g] = 1.0 / den_s[slot, g]

    def finalize_stage(c, slot):
        q0 = c * CHUNK if isinstance(c, int) else pl.multiple_of(c * CHUNK, CHUNK)
        scales = []
        for g in groups:
            inv_den = inv_s[slot, g, 0:1, :]
            scale = None
            for half in range(gw // LANES):
                col = jnp.broadcast_to(inv_den[:, half * LANES:(half + 1) * LANES], (LANES, LANES)).T
                for j in range(LANES // CHUNK):
                    hj = half * (LANES // CHUNK) + j
                    blk = jnp.concatenate([col[j * CHUNK:(j + 1) * CHUNK]] * (gw // LANES), axis=1)
                    scale = blk if scale is None else jnp.where(lane_head == hj, blk, scale)
            scales.append(scale)
        att = o_s[slot] * jnp.concatenate(scales, axis=-1)
        o_ref[pl.ds(q0, CHUNK), :] = (_rms(att) * g_ref[...]).astype(o_ref.dtype)

    def block(c, slot, first=0, last=n_chunks - 1):
        if first <= c <= last:
            scores_stage(c, slot)
        if first <= c - 1 <= last:
            softmax_stage(1 - slot)
        if first <= c - 2 <= last:
            pv_stage(c - 2, slot)
        if first <= c - 3 <= last:
            finalize_stage(c - 3, 1 - slot)
        if first <= c <= last:
            mask_stage(c, slot)

    def steady_pair(j, carry):
        c = 3 + 2 * j
        for k in range(2):
            scores_stage(c + k, (1 + k) % 2)
            softmax_stage(k % 2)
            pv_stage(c + k - 2, (1 + k) % 2)
            finalize_stage(c + k - 3, k % 2)
            mask_stage(c + k, (1 + k) % 2)
        return carry

    n_pairs = (n_chunks - 4) // 2
    assert n_chunks >= 4 and n_chunks % 2 == 0
    for c in range(3):
        block(c, c % 2)
    lax.fori_loop(0, n_pairs, steady_pair, 0)
    for c in range(3 + 2 * n_pairs, n_chunks + 3):
        block(c, c % 2)


def _attention(q, k, v, bias_t, g):
    b, s, w = q.shape
    tq = ATT_TQ
    assert tq % ATT_LEFT == 0
    cur = lambda bi, i: (bi, i, 0)
    prev = lambda bi, i: (bi, jnp.maximum(i * (tq // ATT_LEFT) - 1, 0), 0)
    blk = (None, tq, w)
    left = (None, ATT_LEFT, w)
    return pl.pallas_call(
        _attn_kernel,
        grid=(b, s // tq),
        in_specs=[pl.BlockSpec(blk, cur),
                  pl.BlockSpec(left, prev), pl.BlockSpec(blk, cur),
                  pl.BlockSpec(left, prev), pl.BlockSpec(blk, cur),
                  _resident(bias_t.shape), _resident(g.shape)],
        out_specs=pl.BlockSpec(blk, cur),
        out_shape=jax.ShapeDtypeStruct((b, s, w), BF16),
        scratch_shapes=[pltpu.VMEM((ATT_LEFT + tq, w), BF16),
                        pltpu.VMEM((ATT_LEFT + tq + BAND_PAD - BAND, w), BF16),
                        pltpu.VMEM((2, ATT_GROUPS, BAND, ATT_GROUP_WIDTH), F32),
                        pltpu.VMEM((2, ATT_GROUPS, ATT_GROUP_WIDTH, BAND_PAD), BF16),
                        pltpu.VMEM((2, ATT_GROUPS, 8, ATT_GROUP_WIDTH), F32),
                        pltpu.VMEM((2, ATT_GROUPS, 8, ATT_GROUP_WIDTH), F32),
                        pltpu.VMEM((2, CHUNK, w), F32),
                        pltpu.VMEM((2, ATT_GROUPS, 8, ATT_GROUP_WIDTH), F32)],
        compiler_params=_params("arbitrary", "arbitrary"),
        name="chunk_attn",
    )(q, k, k, v, v, bias_t, g)


def _bias_table(rel_bias):
    n_rel = rel_bias.shape[1]
    n_f = BAND + REL_FUTURE
    f = jnp.concatenate([jnp.broadcast_to(rel_bias[:, n_rel - 1:], (ATT_HEADS, n_f - n_rel)),
                         rel_bias[:, ::-1]], axis=1).astype(F32) * LOG2E
    skew = jnp.broadcast_to(jnp.pad(f, ((0, 0), (0, 1)))[:, None, :], (ATT_HEADS, CHUNK, n_f + 1))
    skew = skew.reshape(ATT_HEADS, CHUNK * (n_f + 1))[:, :CHUNK * n_f].reshape(ATT_HEADS, CHUNK, n_f)
    bias = skew[:, :, REL_FUTURE:]
    bias = bias.reshape(ATT_GROUPS, ATT_GROUP_HEADS, CHUNK, BAND)
    return jnp.transpose(bias, (0, 3, 1, 2)).reshape(ATT_GROUPS, BAND, ATT_GROUP_WIDTH)


_HALO = 8


def _ssd_kernel(xbc_ref, z_ref, dt_ref, convw_ref, convb_ref, dtb_ref, a_ref, dskip_ref,
                expand_ref, g_ref, o_ref, raw_s, act_s, dt_s, y_s, h_s):
    i = pl.program_id(1)
    tb = SSD_TB
    L = SSD_L

    @pl.when(i == 0)
    def _():
        raw_s[0:_HALO, :] = jnp.zeros((_HALO, XBC_WIDTH), F32)
        h_s[...] = jnp.zeros_like(h_s)

    @pl.when(i > 0)
    def _():
        raw_s[0:_HALO, :] = raw_s[tb:tb + _HALO, :]

    raw_s[_HALO:, :] = xbc_ref[...].astype(F32)
    assert SSD_CONV == 4
    raw = raw_s[...]
    back1 = pltpu.roll(raw, 1, axis=0)
    near = convw_ref[3:4, :] * raw + convw_ref[2:3, :] * back1
    far = convw_ref[1:2, :] * raw + convw_ref[0:1, :] * back1
    conv = convb_ref[...] + near[_HALO:] + pltpu.roll(far, 2, axis=0)[_HALO:]
    act_s[...] = _silu(conv)

    dtr = dt_ref[...] + dtb_ref[...]
    dt_s[...] = jnp.maximum(dtr, 0.0) + jnp.log1p(jnp.exp(-jnp.abs(dtr)))

    a = -jnp.exp(a_ref[...])
    ri = lax.broadcasted_iota(jnp.int32, (L, L), 0)
    ci = lax.broadcasted_iota(jnp.int32, (L, L), 1)
    causal = ri >= ci
    tril = causal.astype(BF16)
    low_lanes = lax.broadcasted_iota(jnp.int32, (L, LANES), 1) < SSD_HEAD_DIM
    expand = expand_ref[...]
    dskip = dskip_ref[...]
    bc0 = SSD_WIDTH
    cc0 = SSD_WIDTH + SSD_GROUPS * SSD_STATE

    def chunk_body(c, carry):
        r0 = pl.multiple_of(c * L, L)
        dtc = dt_s[pl.ds(r0, L), :]
        adt = dtc * a
        cs3 = jnp.dot(tril, jnp.concatenate(_split_bf16(adt, 3), axis=1), preferred_element_type=F32)
        cs = cs3[:, 0:LANES] + cs3[:, LANES:2 * LANES] + cs3[:, 2 * LANES:]
        cs_t = cs.T
        last = cs[L - 1:L, :]
        stacked = jnp.concatenate([dtc, jnp.exp(last - cs), jnp.exp(cs)], axis=0)
        wide = jnp.dot(jnp.concatenate(_split_bf16(stacked, 2), axis=1), expand,
                       preferred_element_type=F32)
        dt_w, decay_w, ecs_w = wide[0:L], wide[L:2 * L], wide[2 * L:3 * L]
        cdec = jnp.broadcast_to(jnp.exp(last), (8, LANES))
        cdec_w = sum(jnp.dot(part, expand[:LANES], preferred_element_type=F32)
                     for part in _split_bf16(cdec, 3))[0:1]

        xs = act_s[pl.ds(r0, L), 0:SSD_WIDTH]
        xdt = xs * dt_w
        xdt_b = xdt.astype(BF16)
        xdec_b = (xdt * decay_w).astype(BF16)
        ys = []
        for g in range(SSD_GROUPS):
            gl = slice(g * GROUP_WIDTH, (g + 1) * GROUP_WIDTH)
            bm = act_s[pl.ds(r0, L), bc0 + g * SSD_STATE:bc0 + (g + 1) * SSD_STATE].astype(BF16)
            cm = act_s[pl.ds(r0, L), cc0 + g * SSD_STATE:cc0 + (g + 1) * SSD_STATE].astype(BF16)
            cb = lax.dot_general(cm, bm, (((1,), (1,)), ((), ())), preferred_element_type=F32)
            hprev = h_s[g]
            y_off = jnp.dot(cm, hprev.astype(BF16), preferred_element_type=F32) * ecs_w[:, gl]
            states = lax.dot_general(bm, xdec_b[:, gl], (((0,), (0,)), ((), ())),
                                     preferred_element_type=F32)
            h_s[g] = hprev * cdec_w[:, gl] + states
            yd = []
            for pr in range(GROUP_WIDTH // LANES):
                hp = g * (GROUP_WIDTH // LANES) + pr
                xp = xdt_b[:, hp * LANES:(hp + 1) * LANES]
                halves = []
                for hh in (2 * hp, 2 * hp + 1):
                    seg = jnp.exp(jnp.where(causal, cs[:, hh:hh + 1] - cs_t[hh:hh + 1, :], -jnp.inf))
                    halves.append(jnp.dot((cb * seg).astype(BF16), xp, preferred_element_type=F32))
                yd.append(jnp.where(low_lanes, halves[0], halves[1]))
            ys.append(jnp.concatenate(yd, axis=-1) + y_off)
        y = jnp.concatenate(ys, axis=-1) + xs * dskip
        y_s[pl.ds(r0, L), :] = y
        return carry

    lax.fori_loop(0, tb // L, chunk_body, 0)
    gated = y_s[...] * _silu(z_ref[...].astype(F32))
    o_ref[...] = (_rms(gated) * g_ref[...]).astype(o_ref.dtype)


def _ssd(xbc, z, dt_raw, conv_w, conv_b, dt_bias, a_log, dskip_w, expand, g):
    b, s, _ = xbc.shape
    tb = SSD_TB
    row = lambda bi, i: (bi, i, 0)
    return pl.pallas_call(
        _ssd_kernel,
        grid=(b, s // tb),
        in_specs=[pl.BlockSpec((None, tb, XBC_WIDTH), row),
                  pl.BlockSpec((None, tb, SSD_WIDTH), row),
                  pl.BlockSpec((None, tb, LANES), row),
                  _resident(conv_w.shape), _resident(conv_b.shape), _resident(dt_bias.shape),
                  _resident(a_log.shape), _resident(dskip_w.shape), _resident(expand.shape),
                  _resident(g.shape)],
        out_specs=pl.BlockSpec((None, tb, SSD_WIDTH), row),
        out_shape=jax.ShapeDtypeStruct((b, s, SSD_WIDTH), BF16),
        scratch_shapes=[pltpu.VMEM((_HALO + tb, XBC_WIDTH), F32),
                        pltpu.VMEM((tb, XBC_WIDTH), F32),
                        pltpu.VMEM((tb, LANES), F32),
                        pltpu.VMEM((tb, SSD_WIDTH), F32),
                        pltpu.VMEM((SSD_GROUPS, SSD_STATE, GROUP_WIDTH), F32)],
        compiler_params=_params("arbitrary", "arbitrary"),
        name="ssd",
    )(xbc, z, dt_raw, conv_w, conv_b, dt_bias, a_log, dskip_w, expand, g)


def _outproj_kernel(att_ref, ssd_ref, x_ref, mods_ref, g_ref, w_ref, x1_ref, h_ref):
    for r0 in range(0, x_ref.shape[0], OUT_SUB):
        rows = slice(r0, r0 + OUT_SUB)
        mix = jnp.dot(att_ref[rows, :], w_ref[0:ATT_WIDTH, :], preferred_element_type=F32)
        mix = mix + jnp.dot(ssd_ref[rows, :], w_ref[ATT_WIDTH:, :], preferred_element_type=F32)
        x1 = x_ref[rows, :] + mods_ref[2:3, :] * mix
        x1_ref[rows, :] = x1
        h = _rms(x1) * g_ref[...]
        h_ref[rows, :] = (h * (1.0 + mods_ref[4:5, :]) + mods_ref[3:4, :]).astype(h_ref.dtype)


def _outproj(att, ssd, x, mods, g, w):
    b, s, d = x.shape
    tm = OUT_TM
    row = lambda bi, i: (bi, i, 0)
    return pl.pallas_call(
        _outproj_kernel,
        grid=(b, s // tm),
        in_specs=[pl.BlockSpec((None, tm, ATT_WIDTH), row),
                  pl.BlockSpec((None, tm, SSD_WIDTH), row),
                  pl.BlockSpec((None, tm, d), row),
                  pl.BlockSpec((None, N_MOD, d), lambda bi, i: (bi, 0, 0)),
                  _resident(g.shape), _resident(w.shape)],
        out_specs=[pl.BlockSpec((None, tm, d), row), pl.BlockSpec((None, tm, d), row)],
        out_shape=[jax.ShapeDtypeStruct((b, s, d), F32), jax.ShapeDtypeStruct((b, s, d), BF16)],
        compiler_params=_params("arbitrary", "arbitrary"),
        name="outproj",
    )(att, ssd, x, mods, g, w)


def _ffn_kernel(final_norm, h_ref, x1_ref, mods_ref, g_ref, wg_ref, wu_ref, wd_ref, o_ref):
    k = pl.program_id(2)

    @pl.when(k == 0)
    def _():
        o_ref[...] = x1_ref[...]

    h = h_ref[...]
    gate = jnp.dot(h, wg_ref[...], preferred_element_type=F32)
    up = jnp.dot(h, wu_ref[...], preferred_element_type=F32)
    act = (_silu(gate) * up).astype(BF16)
    d = o_ref.shape[-1]
    for n0 in range(0, d, FFN_TN):
        cols = slice(n0, n0 + FFN_TN)
        o_ref[:, cols] += mods_ref[5:6, cols] * jnp.dot(act, wd_ref[:, cols], preferred_element_type=F32)

    if final_norm:
        @pl.when(k == pl.num_programs(2) - 1)
        def _():
            o_ref[...] = _rms(o_ref[...]) * g_ref[...]


def _ffn(h, x1, mods, g_final, w_gate, w_up, w_down):
    b, s, d = x1.shape
    tm, th = FFN_TM, FFN_TH
    hid = w_gate.shape[1]
    row = lambda bi, i, k: (bi, i, 0)
    final_norm = g_final is not None
    g = g_final if final_norm else jnp.ones((1, d), F32)
    return pl.pallas_call(
        functools.partial(_ffn_kernel, final_norm),
        grid=(b, s // tm, hid // th),
        in_specs=[pl.BlockSpec((None, tm, d), row),
                  pl.BlockSpec((None, tm, d), row),
                  pl.BlockSpec((None, N_MOD, d), lambda bi, i, k: (bi, 0, 0)),
                  pl.BlockSpec(g.shape, lambda bi, i, k: (0, 0)),
                  pl.BlockSpec((d, th), lambda bi, i, k: (0, k)),
                  pl.BlockSpec((d, th), lambda bi, i, k: (0, k)),
                  pl.BlockSpec((th, d), lambda bi, i, k: (k, 0))],
        out_specs=pl.BlockSpec((None, tm, d), row),
        out_shape=jax.ShapeDtypeStruct((b, s, d), F32),
        compiler_params=_params("arbitrary", "arbitrary", "arbitrary"),
        name="ffn",
    )(h, x1, mods, g, w_gate, w_up, w_down)


def _pad_lanes(v):
    return jnp.pad(v.astype(F32), (0, LANES - v.shape[0])).reshape(1, LANES)


def kernel(x, c, w_ada, b_ada, g_mix, w_in, rel_bias, conv_w, conv_b, dt_bias, a_log, d_skip,
           g_att_out, g_ssd_out, w_out, g_ffn, w_gate, w_up, w_down, g_final):
    b, s, d = x.shape
    depth = w_ada.shape[0]
    n_main = sum(_PROJ_SPLITS)
    head_of_lane = jnp.arange(SSD_WIDTH) // SSD_HEAD_DIM
    expand = (jnp.arange(2 * LANES)[:, None] % LANES == head_of_lane[None, :]).astype(BF16)
    c_pad = jnp.pad(c, ((0, 8 - b % 8 if b % 8 else 0), (0, 0)))
    for l in range(depth):
        mods = _adaln(c_pad, w_ada[l], b_ada[l].reshape(1, -1))[:b].reshape(b, N_MOD, d)
        n_in = w_in.shape[-1]
        q_scale = jnp.where(jnp.arange(n_in) < ATT_WIDTH, ATT_HEAD_DIM ** -0.5 * LOG2E, 1.0).astype(F32)
        w_t = w_in[l].T
        w_proj = (w_t * q_scale[:, None]).astype(BF16)
        w_dt = jnp.pad(w_proj[n_main:], ((0, LANES - (n_in - n_main)), (0, 0)))
        q, k, v, z, xbc, dt_raw = _inproj(x, mods, g_mix[l].reshape(1, d), w_proj, w_dt)
        att = _attention(q, k, v, _bias_table(rel_bias[l]), g_att_out[l].reshape(1, -1))
        ssd = _ssd(xbc, z, dt_raw, conv_w[l], conv_b[l].reshape(1, -1), _pad_lanes(dt_bias[l]),
                   _pad_lanes(a_log[l]), d_skip[l].astype(F32)[head_of_lane].reshape(1, -1), expand,
                   g_ssd_out[l].reshape(1, -1))
        x1, h2 = _outproj(att, ssd, x, mods, g_ffn[l].reshape(1, d), w_out[l].astype(BF16))
        g_last = g_final.reshape(1, d) if l == depth - 1 else None
        x = _ffn(h2, x1, mods, g_last, w_gate[l].astype(BF16), w_up[l].astype(BF16),
                 w_down[l].astype(BF16))
    return x
```

```python
import functools

import jax
import jax.numpy as jnp
from jax import lax
from jax.experimental import pallas as pl
from jax.experimental.pallas import tpu as pltpu

F32 = jnp.float32
BF16 = jnp.bfloat16

D_MODEL = 2048
CHUNK = 64
LEFT_CHUNKS = 8
BAND = (LEFT_CHUNKS + 1) * CHUNK
ATT_HEADS = 16
ATT_HEAD_DIM = 64
ATT_WIDTH = ATT_HEADS * ATT_HEAD_DIM
REL_CLIP = 256
REL_FUTURE = CHUNK - 1
SSD_HEADS = 16
SSD_HEAD_DIM = 64
SSD_WIDTH = SSD_HEADS * SSD_HEAD_DIM
SSD_GROUPS = 2
SSD_STATE = 128
SSD_CONV = 4
XBC_WIDTH = SSD_WIDTH + 2 * SSD_GROUPS * SSD_STATE
GROUP_WIDTH = SSD_WIDTH // SSD_GROUPS
FFN_HIDDEN = 5632
N_MOD = 6
EPS = 1e-6

LANES = 128
BF16_SUBLANES = 16
MXU_WIDTH = 256
ATT_GROUP_WIDTH = MXU_WIDTH
ATT_GROUP_HEADS = ATT_GROUP_WIDTH // ATT_HEAD_DIM
ATT_GROUPS = ATT_WIDTH // ATT_GROUP_WIDTH
BAND_PAD = -(-BAND // LANES) * LANES
VMEM_LIMIT = 60 * 1024 * 1024

ADA_TN = 1024
PROJ_TM = 512
PROJ_TN = 512
ATT_TQ = 1024
ATT_LEFT = LEFT_CHUNKS * CHUNK
SSD_TB = 512
SSD_L = 256
OUT_TM = 512
OUT_SUB = 256
FFN_TM = 1024
FFN_TH = 512
FFN_TN = 512

HIGHEST = lax.Precision.HIGHEST


def _params(*sem):
    return pltpu.CompilerParams(dimension_semantics=sem, vmem_limit_bytes=VMEM_LIMIT)


def _resident(shape):
    nd = len(shape)
    return pl.BlockSpec(shape, lambda *_: (0,) * nd, pipeline_mode=pl.Buffered(1))


def _rms(x):
    return x * lax.rsqrt(jnp.mean(x * x, axis=-1, keepdims=True) + EPS)


def _silu(x):
    h = 0.5 * x
    return h + h * jnp.tanh(h)


def _split_bf16(x, terms):
    parts = []
    for _ in range(terms):
        p = x.astype(BF16)
        parts.append(p)
        x = x - p.astype(F32)
    return parts


def _adaln_kernel(c_ref, w_ref, b_ref, o_ref):
    cond = _silu(c_ref[...])
    o_ref[...] = jnp.dot(cond.astype(BF16), w_ref[...].astype(BF16),
                         preferred_element_type=F32) + b_ref[...]


def _adaln(c_pad, w, b):
    rows, d = c_pad.shape
    n = w.shape[1]
    return pl.pallas_call(
        _adaln_kernel,
        grid=(n // ADA_TN,),
        in_specs=[pl.BlockSpec((rows, d), lambda j: (0, 0)),
                  pl.BlockSpec((d, ADA_TN), lambda j: (0, j)),
                  pl.BlockSpec((1, ADA_TN), lambda j: (0, j))],
        out_specs=pl.BlockSpec((rows, ADA_TN), lambda j: (0, j)),
        out_shape=jax.ShapeDtypeStruct((rows, n), F32),
        compiler_params=_params("arbitrary"),
        name="adaln",
    )(c_pad, w, b)


_PROJ_SPLITS = (ATT_WIDTH, ATT_WIDTH, ATT_WIDTH, SSD_WIDTH, XBC_WIDTH)
_NT = (((1,), (1,)), ((), ()))


def _cast_rows(n_rows, n_steps):
    rows = next(r for r in range(BF16_SUBLANES, n_rows + 1, BF16_SUBLANES)
                if n_rows % r == 0 and r * n_steps >= n_rows)
    return rows


def _inproj_kernel(n_cast, x_ref, mods_ref, g_ref, w_ref, wdt_ref, *refs):
    cast_in, refs = refs[:n_cast], refs[n_cast:]
    (q_ref, k_ref, v_ref, z_ref, xbc_ref, dt_ref), refs = refs[:6], refs[6:]
    cast_out, (h_s,) = refs[:n_cast], refs[n_cast:]
    for src, dst in zip(cast_in, cast_out):
        dst[...] = src[...].astype(dst.dtype)
    x = x_ref[...]
    h = _rms(x) * g_ref[...]
    h = h * (1.0 + mods_ref[1:2, :]) + mods_ref[0:1, :]
    h_s[...] = h.astype(BF16)
    col = 0
    for ref, width in zip((q_ref, k_ref, v_ref, z_ref, xbc_ref), _PROJ_SPLITS):
        for j in range(0, width, PROJ_TN):
            ref[:, j:j + PROJ_TN] = lax.dot_general(
                h_s[...], w_ref[col + j:col + j + PROJ_TN, :], _NT,
                preferred_element_type=F32).astype(ref.dtype)
        col += width
    dt_ref[...] = lax.dot_general(h_s[...], wdt_ref[...], _NT, preferred_element_type=F32)


def _inproj(x, mods, g, w, w_dt, to_cast):
    b, s, d = x.shape
    tm = PROJ_TM
    steps_per_seq = s // tm
    row = lambda bi, i: (bi, i, 0)
    out_shape = [jax.ShapeDtypeStruct((b, s, w), BF16) for w in _PROJ_SPLITS]
    out_shape.append(jax.ShapeDtypeStruct((b, s, LANES), F32))
    out_specs = [pl.BlockSpec((None, tm, w), row) for w in _PROJ_SPLITS]
    out_specs.append(pl.BlockSpec((None, tm, LANES), row))
    cast_specs = []
    for a in to_cast:
        rows = _cast_rows(a.shape[0], b * steps_per_seq)
        blk = functools.partial(lambda n_blk, bi, i: (jnp.minimum(bi * steps_per_seq + i, n_blk - 1), 0),
                                a.shape[0] // rows)
        cast_specs.append(pl.BlockSpec((rows, a.shape[1]), blk))
        out_shape.append(jax.ShapeDtypeStruct(a.shape, BF16))
    return pl.pallas_call(
        functools.partial(_inproj_kernel, len(to_cast)),
        grid=(b, steps_per_seq),
        in_specs=[pl.BlockSpec((None, tm, d), row),
                  pl.BlockSpec((None, N_MOD, d), lambda bi, i: (bi, 0, 0)),
                  _resident(g.shape), _resident(w.shape), _resident(w_dt.shape)] + cast_specs,
        out_specs=out_specs + cast_specs,
        out_shape=out_shape,
        scratch_shapes=[pltpu.VMEM((tm, d), BF16)],
        compiler_params=_params("arbitrary", "arbitrary"),
        name="inproj",
    )(x, mods, g, w, w_dt, *to_cast)


LOG2E = 1.4426950408889634


def _attn_kernel(q_ref, kp_ref, kc_ref, vp_ref, vc_ref, bias_ref, g_ref, o_ref,
                 k_s, v_s, s_s, pt_s, max_s, den_s, o_s, inv_s):
    i = pl.program_id(1)
    tq = ATT_TQ
    k_s[0:ATT_LEFT, :] = kp_ref[...]
    k_s[ATT_LEFT:, :] = kc_ref[...]
    v_s[0:ATT_LEFT, :] = vp_ref[...]
    v_s[ATT_LEFT:ATT_LEFT + tq, :] = vc_ref[...]
    v_s[ATT_LEFT + tq:, :] = jnp.zeros((BAND_PAD - BAND, ATT_WIDTH), BF16)

    gw = ATT_GROUP_WIDTH
    rr = lax.broadcasted_iota(jnp.int32, (gw, gw), 0) // CHUNK
    ll = lax.broadcasted_iota(jnp.int32, (gw, gw), 1) // ATT_HEAD_DIM
    own_head = rr == ll
    lane_head = lax.broadcasted_iota(jnp.int32, (CHUNK, gw), 1) // ATT_HEAD_DIM
    groups = range(ATT_GROUPS)
    n_chunks = tq // CHUNK

    def scores_stage(c, slot):
        q0 = c * CHUNK if isinstance(c, int) else pl.multiple_of(c * CHUNK, CHUNK)
        for g in groups:
            lanes = slice(g * gw, (g + 1) * gw)
            qg = q_ref[pl.ds(q0, CHUNK), lanes]
            q_bd = jnp.where(own_head, jnp.concatenate([qg] * ATT_GROUP_HEADS, axis=0), jnp.zeros((), BF16))
            kb = k_s[pl.ds(q0, BAND), lanes]
            st = lax.dot_general(kb, q_bd, (((1,), (1,)), ((), ())),
                                 preferred_element_type=F32) + bias_ref[g]
            s_s[slot, g] = st
            max_s[slot, g] = jnp.broadcast_to(jnp.max(st, axis=0, keepdims=True), (8, gw))

    def mask_stage(c, slot):
        n_before = jnp.maximum(LEFT_CHUNKS - (i * n_chunks + c), 0)

        def mask_slab(j, carry):
            r0 = pl.multiple_of(j * CHUNK, CHUNK)
            for g in groups:
                s_s[slot, g, pl.ds(r0, CHUNK), :] = jnp.full((CHUNK, gw), -jnp.inf, F32)
            return carry

        lax.fori_loop(0, n_before, mask_slab, 0)

        @pl.when(n_before > 0)
        def _():
            for g in groups:
                max_s[slot, g] = jnp.broadcast_to(jnp.max(s_s[slot, g], axis=0, keepdims=True), (8, gw))

    def softmax_stage(slot):
        for g in groups:
            e = jnp.exp2(s_s[slot, g] - max_s[slot, g, 0:1, :])
            den_s[slot, g] = jnp.broadcast_to(jnp.sum(e, axis=0, keepdims=True), (8, gw))
            eb = jnp.concatenate([e.astype(BF16), jnp.zeros((BAND_PAD - BAND, gw), BF16)], axis=0)
            for r in range(0, BAND_PAD, LANES):
                pt_s[slot, g, :, r:r + LANES] = eb[r:r + LANES, :].T

    def pv_stage(c, slot):
        q0 = c * CHUNK if isinstance(c, int) else pl.multiple_of(c * CHUNK, CHUNK)
        for g in groups:
            lanes = slice(g * gw, (g + 1) * gw)
            vb = v_s[pl.ds(q0, BAND_PAD), lanes]
            o4 = jnp.dot(pt_s[slot, g], vb, preferred_element_type=F32)
            out = o4[0:CHUNK]
            for hj in range(1, ATT_GROUP_HEADS):
                out = jnp.where(lane_head == hj, o4[hj * CHUNK:(hj + 1) * CHUNK], out)
            o_s[slot, :, lanes] = out
            inv_s[slot, g] = 1.0 / den_s[slot, g]

    def finalize_stage(c, slot):
        q0 = c * CHUNK if isinstance(c, int) else pl.multiple_of(c * CHUNK, CHUNK)
        scales = []
        for g in groups:
            inv_den = inv_s[slot, g, 0:1, :]
            scale = None
            for half in range(gw // LANES):
                col = jnp.broadcast_to(inv_den[:, half * LANES:(half + 1) * LANES], (LANES, LANES)).T
                for j in range(LANES // CHUNK):
                    hj = half * (LANES // CHUNK) + j
                    blk = jnp.concatenate([col[j * CHUNK:(j + 1) * CHUNK]] * (gw // LANES), axis=1)
                    scale = blk if scale is None else jnp.where(lane_head == hj, blk, scale)
            scales.append(scale)
        att = o_s[slot] * jnp.concatenate(scales, axis=-1)
        o_ref[pl.ds(q0, CHUNK), :] = (_rms(att) * g_ref[...]).astype(o_ref.dtype)

    def block(c, slot, first=0, last=n_chunks - 1):
        if first <= c <= last:
            scores_stage(c, slot)
        if first <= c - 1 <= last:
            softmax_stage(1 - slot)
        if first <= c - 2 <= last:
            pv_stage(c - 2, slot)
        if first <= c - 3 <= last:
            finalize_stage(c - 3, 1 - slot)
        if first <= c <= last:
            mask_stage(c, slot)

    def steady_pair(j, carry):
        c = 3 + 2 * j
        for k in range(2):
            scores_stage(c + k, (1 + k) % 2)
            softmax_stage(k % 2)
            pv_stage(c + k - 2, (1 + k) % 2)
            finalize_stage(c + k - 3, k % 2)
            mask_stage(c + k, (1 + k) % 2)
        return carry

    n_pairs = (n_chunks - 4) // 2
    assert n_chunks >= 4 and n_chunks % 2 == 0
    for c in range(3):
        block(c, c % 2)
    lax.fori_loop(0, n_pairs, steady_pair, 0)
    for c in range(3 + 2 * n_pairs, n_chunks + 3):
        block(c, c % 2)


def _attention(q, k, v, bias_t, g):
    b, s, w = q.shape
    tq = ATT_TQ
    assert tq % ATT_LEFT == 0
    cur = lambda bi, i: (bi, i, 0)
    prev = lambda bi, i: (bi, jnp.maximum(i * (tq // ATT_LEFT) - 1, 0), 0)
    blk = (None, tq, w)
    left = (None, ATT_LEFT, w)
    return pl.pallas_call(
        _attn_kernel,
        grid=(b, s // tq),
        in_specs=[pl.BlockSpec(blk, cur),
                  pl.BlockSpec(left, prev), pl.BlockSpec(blk, cur),
                  pl.BlockSpec(left, prev), pl.BlockSpec(blk, cur),
                  _resident(bias_t.shape), _resident(g.shape)],
        out_specs=pl.BlockSpec(blk, cur),
        out_shape=jax.ShapeDtypeStruct((b, s, w), BF16),
        scratch_shapes=[pltpu.VMEM((ATT_LEFT + tq, w), BF16),
                        pltpu.VMEM((ATT_LEFT + tq + BAND_PAD - BAND, w), BF16),
                        pltpu.VMEM((2, ATT_GROUPS, BAND, ATT_GROUP_WIDTH), F32),
                        pltpu.VMEM((2, ATT_GROUPS, ATT_GROUP_WIDTH, BAND_PAD), BF16),
                        pltpu.VMEM((2, ATT_GROUPS, 8, ATT_GROUP_WIDTH), F32),
                        pltpu.VMEM((2, ATT_GROUPS, 8, ATT_GROUP_WIDTH), F32),
                        pltpu.VMEM((2, CHUNK, w), F32),
                        pltpu.VMEM((2, ATT_GROUPS, 8, ATT_GROUP_WIDTH), F32)],
        compiler_params=_params("arbitrary", "arbitrary"),
        name="chunk_attn",
    )(q, k, k, v, v, bias_t, g)


def _bias_table(rel_bias):
    n_rel = rel_bias.shape[1]
    n_f = BAND + REL_FUTURE
    f = jnp.concatenate([jnp.broadcast_to(rel_bias[:, n_rel - 1:], (ATT_HEADS, n_f - n_rel)),
                         rel_bias[:, ::-1]], axis=1).astype(F32) * LOG2E
    skew = jnp.broadcast_to(jnp.pad(f, ((0, 0), (0, 1)))[:, None, :], (ATT_HEADS, CHUNK, n_f + 1))
    skew = skew.reshape(ATT_HEADS, CHUNK * (n_f + 1))[:, :CHUNK * n_f].reshape(ATT_HEADS, CHUNK, n_f)
    bias = skew[:, :, REL_FUTURE:]
    bias = bias.reshape(ATT_GROUPS, ATT_GROUP_HEADS, CHUNK, BAND)
    return jnp.transpose(bias, (0, 3, 1, 2)).reshape(ATT_GROUPS, BAND, ATT_GROUP_WIDTH)


_HALO = 8


def _ssd_kernel(xbc_ref, z_ref, dt_ref, convw_ref, convb_ref, dtb_ref, a_ref, dskip_ref,
                expand_ref, g_ref, o_ref, raw_s, act_s, dt_s, y_s, h_s):
    i = pl.program_id(1)
    tb = SSD_TB
    L = SSD_L

    @pl.when(i == 0)
    def _():
        raw_s[0:_HALO, :] = jnp.zeros((_HALO, XBC_WIDTH), F32)
        h_s[...] = jnp.zeros_like(h_s)

    @pl.when(i > 0)
    def _():
        raw_s[0:_HALO, :] = raw_s[tb:tb + _HALO, :]

    raw_s[_HALO:, :] = xbc_ref[...].astype(F32)
    assert SSD_CONV == 4
    raw = raw_s[...]
    back1 = pltpu.roll(raw, 1, axis=0)
    near = convw_ref[3:4, :] * raw + convw_ref[2:3, :] * back1
    far = convw_ref[1:2, :] * raw + convw_ref[0:1, :] * back1
    conv = convb_ref[...] + near[_HALO:] + pltpu.roll(far, 2, axis=0)[_HALO:]
    act_s[...] = _silu(conv)

    dtr = dt_ref[...] + dtb_ref[...]
    dt_s[...] = jnp.maximum(dtr, 0.0) + jnp.log1p(jnp.exp(-jnp.abs(dtr)))

    a = -jnp.exp(a_ref[...])
    ri = lax.broadcasted_iota(jnp.int32, (L, L), 0)
    ci = lax.broadcasted_iota(jnp.int32, (L, L), 1)
    causal = ri >= ci
    tril = causal.astype(BF16)
    low_lanes = lax.broadcasted_iota(jnp.int32, (L, LANES), 1) < SSD_HEAD_DIM
    expand = expand_ref[...]
    dskip = dskip_ref[...]
    bc0 = SSD_WIDTH
    cc0 = SSD_WIDTH + SSD_GROUPS * SSD_STATE

    def chunk_body(c, carry):
        r0 = pl.multiple_of(c * L, L)
        dtc = dt_s[pl.ds(r0, L), :]
        adt = dtc * a
        cs3 = jnp.dot(tril, jnp.concatenate(_split_bf16(adt, 3), axis=1), preferred_element_type=F32)
        cs = cs3[:, 0:LANES] + cs3[:, LANES:2 * LANES] + cs3[:, 2 * LANES:]
        cs_t = cs.T
        last = cs[L - 1:L, :]
        stacked = jnp.concatenate([dtc, jnp.exp(last - cs), jnp.exp(cs)], axis=0)
        wide = jnp.dot(jnp.concatenate(_split_bf16(stacked, 2), axis=1), expand,
                       preferred_element_type=F32)
        dt_w, decay_w, ecs_w = wide[0:L], wide[L:2 * L], wide[2 * L:3 * L]
        cdec = jnp.broadcast_to(jnp.exp(last), (8, LANES))
        cdec_w = sum(jnp.dot(part, expand[:LANES], preferred_element_type=F32)
                     for part in _split_bf16(cdec, 3))[0:1]

        xs = act_s[pl.ds(r0, L), 0:SSD_WIDTH]
        xdt = xs * dt_w
        xdt_b = xdt.astype(BF16)
        xdec_b = (xdt * decay_w).astype(BF16)
        ys = []
        for g in range(SSD_GROUPS):
            gl = slice(g * GROUP_WIDTH, (g + 1) * GROUP_WIDTH)
            bm = act_s[pl.ds(r0, L), bc0 + g * SSD_STATE:bc0 + (g + 1) * SSD_STATE].astype(BF16)
            cm = act_s[pl.ds(r0, L), cc0 + g * SSD_STATE:cc0 + (g + 1) * SSD_STATE].astype(BF16)
            cb = lax.dot_general(cm, bm, (((1,), (1,)), ((), ())), preferred_element_type=F32)
            hprev = h_s[g]
            y_off = jnp.dot(cm, hprev.astype(BF16), preferred_element_type=F32) * ecs_w[:, gl]
            states = lax.dot_general(bm, xdec_b[:, gl], (((0,), (0,)), ((), ())),
                                     preferred_element_type=F32)
            h_s[g] = hprev * cdec_w[:, gl] + states
            yd = []
            for pr in range(GROUP_WIDTH // LANES):
                hp = g * (GROUP_WIDTH // LANES) + pr
                xp = xdt_b[:, hp * LANES:(hp + 1) * LANES]
                halves = []
                for hh in (2 * hp, 2 * hp + 1):
                    seg = jnp.exp(jnp.where(causal, cs[:, hh:hh + 1] - cs_t[hh:hh + 1, :], -jnp.inf))
                    halves.append(jnp.dot((cb * seg).astype(BF16), xp, preferred_element_type=F32))
                yd.append(jnp.where(low_lanes, halves[0], halves[1]))
            ys.append(jnp.concatenate(yd, axis=-1) + y_off)
        y = jnp.concatenate(ys, axis=-1) + xs * dskip
        y_s[pl.ds(r0, L), :] = y
        return carry

    lax.fori_loop(0, tb // L, chunk_body, 0)
    gated = y_s[...] * _silu(z_ref[...].astype(F32))
    o_ref[...] = (_rms(gated) * g_ref[...]).astype(o_ref.dtype)


def _ssd(xbc, z, dt_raw, conv_w, conv_b, dt_bias, a_log, dskip_w, expand, g):
    b, s, _ = xbc.shape
    tb = SSD_TB
    row = lambda bi, i: (bi, i, 0)
    return pl.pallas_call(
        _ssd_kernel,
        grid=(b, s // tb),
        in_specs=[pl.BlockSpec((None, tb, XBC_WIDTH), row),
                  pl.BlockSpec((None, tb, SSD_WIDTH), row),
                  pl.BlockSpec((None, tb, LANES), row),
                  _resident(conv_w.shape), _resident(conv_b.shape), _resident(dt_bias.shape),
                  _resident(a_log.shape), _resident(dskip_w.shape), _resident(expand.shape),
                  _resident(g.shape)],
        out_specs=pl.BlockSpec((None, tb, SSD_WIDTH), row),
        out_shape=jax.ShapeDtypeStruct((b, s, SSD_WIDTH), BF16),
        scratch_shapes=[pltpu.VMEM((_HALO + tb, XBC_WIDTH), F32),
                        pltpu.VMEM((tb, XBC_WIDTH), F32),
                        pltpu.VMEM((tb, LANES), F32),
                        pltpu.VMEM((tb, SSD_WIDTH), F32),
                        pltpu.VMEM((SSD_GROUPS, SSD_STATE, GROUP_WIDTH), F32)],
        compiler_params=_params("arbitrary", "arbitrary"),
        name="ssd",
    )(xbc, z, dt_raw, conv_w, conv_b, dt_bias, a_log, dskip_w, expand, g)


def _outproj_kernel(att_ref, ssd_ref, x_ref, mods_ref, g_ref, w_ref, x1_ref, h_ref):
    for r0 in range(0, x_ref.shape[0], OUT_SUB):
        rows = slice(r0, r0 + OUT_SUB)
        mix = jnp.dot(att_ref[rows, :], w_ref[0:ATT_WIDTH, :], preferred_element_type=F32)
        mix = mix + jnp.dot(ssd_ref[rows, :], w_ref[ATT_WIDTH:, :], preferred_element_type=F32)
        x1 = x_ref[rows, :] + mods_ref[2:3, :] * mix
        x1_ref[rows, :] = x1
        h = _rms(x1) * g_ref[...]
        h_ref[rows, :] = (h * (1.0 + mods_ref[4:5, :]) + mods_ref[3:4, :]).astype(h_ref.dtype)


def _outproj(att, ssd, x, mods, g, w):
    b, s, d = x.shape
    tm = OUT_TM
    row = lambda bi, i: (bi, i, 0)
    return pl.pallas_call(
        _outproj_kernel,
        grid=(b, s // tm),
        in_specs=[pl.BlockSpec((None, tm, ATT_WIDTH), row),
                  pl.BlockSpec((None, tm, SSD_WIDTH), row),
                  pl.BlockSpec((None, tm, d), row),
                  pl.BlockSpec((None, N_MOD, d), lambda bi, i: (bi, 0, 0)),
                  _resident(g.shape), _resident(w.shape)],
        out_specs=[pl.BlockSpec((None, tm, d), row), pl.BlockSpec((None, tm, d), row)],
        out_shape=[jax.ShapeDtypeStruct((b, s, d), F32), jax.ShapeDtypeStruct((b, s, d), BF16)],
        compiler_params=_params("arbitrary", "arbitrary"),
        name="outproj",
    )(att, ssd, x, mods, g, w)


def _ffn_kernel(final_norm, h_ref, x1_ref, mods_ref, g_ref, wg_ref, wu_ref, wd_ref, o_ref):
    k = pl.program_id(2)

    @pl.when(k == 0)
    def _():
        o_ref[...] = x1_ref[...]

    h = h_ref[...]
    gate = jnp.dot(h, wg_ref[...], preferred_element_type=F32)
    up = jnp.dot(h, wu_ref[...], preferred_element_type=F32)
    act = (_silu(gate) * up).astype(BF16)
    d = o_ref.shape[-1]
    for n0 in range(0, d, FFN_TN):
        cols = slice(n0, n0 + FFN_TN)
        o_ref[:, cols] += mods_ref[5:6, cols] * jnp.dot(act, wd_ref[:, cols], preferred_element_type=F32)

    if final_norm:
        @pl.when(k == pl.num_programs(2) - 1)
        def _():
            o_ref[...] = _rms(o_ref[...]) * g_ref[...]


def _ffn(h, x1, mods, g_final, w_gate, w_up, w_down):
    b, s, d = x1.shape
    tm, th = FFN_TM, FFN_TH
    hid = w_gate.shape[1]
    row = lambda bi, i, k: (bi, i, 0)
    final_norm = g_final is not None
    g = g_final if final_norm else jnp.ones((1, d), F32)
    return pl.pallas_call(
        functools.partial(_ffn_kernel, final_norm),
        grid=(b, s // tm, hid // th),
        in_specs=[pl.BlockSpec((None, tm, d), row),
                  pl.BlockSpec((None, tm, d), row),
                  pl.BlockSpec((None, N_MOD, d), lambda bi, i, k: (bi, 0, 0)),
                  pl.BlockSpec(g.shape, lambda bi, i, k: (0, 0)),
                  pl.BlockSpec((d, th), lambda bi, i, k: (0, k)),
                  pl.BlockSpec((d, th), lambda bi, i, k: (0, k)),
                  pl.BlockSpec((th, d), lambda bi, i, k: (k, 0))],
        out_specs=pl.BlockSpec((None, tm, d), row),
        out_shape=jax.ShapeDtypeStruct((b, s, d), F32),
        compiler_params=_params("arbitrary", "arbitrary", "arbitrary"),
        name="ffn",
    )(h, x1, mods, g, w_gate, w_up, w_down)


def _pad_lanes(v):
    return jnp.pad(v.astype(F32), (0, LANES - v.shape[0])).reshape(1, LANES)


def kernel(x, c, w_ada, b_ada, g_mix, w_in, rel_bias, conv_w, conv_b, dt_bias, a_log, d_skip,
           g_att_out, g_ssd_out, w_out, g_ffn, w_gate, w_up, w_down, g_final):
    b, s, d = x.shape
    depth = w_ada.shape[0]
    n_main = sum(_PROJ_SPLITS)
    head_of_lane = jnp.arange(SSD_WIDTH) // SSD_HEAD_DIM
    expand = (jnp.arange(2 * LANES)[:, None] % LANES == head_of_lane[None, :]).astype(BF16)
    c_pad = jnp.pad(c, ((0, 8 - b % 8 if b % 8 else 0), (0, 0)))
    for l in range(depth):
        mods = _adaln(c_pad, w_ada[l], b_ada[l].reshape(1, -1))[:b].reshape(b, N_MOD, d)
        n_in = w_in.shape[-1]
        q_scale = jnp.where(jnp.arange(n_in) < ATT_WIDTH, ATT_HEAD_DIM ** -0.5 * LOG2E, 1.0).astype(F32)
        w_t = w_in[l].T
        w_proj = (w_t * q_scale[:, None]).astype(BF16)
        w_dt = jnp.pad(w_proj[n_main:], ((0, LANES - (n_in - n_main)), (0, 0)))
        q, k, v, z, xbc, dt_raw, w_o, w_g, w_u, w_d = _inproj(
            x, mods, g_mix[l].reshape(1, d), w_proj, w_dt, (w_out[l], w_gate[l], w_up[l], w_down[l]))
        att = _attention(q, k, v, _bias_table(rel_bias[l]), g_att_out[l].reshape(1, -1))
        ssd = _ssd(xbc, z, dt_raw, conv_w[l], conv_b[l].reshape(1, -1), _pad_lanes(dt_bias[l]),
                   _pad_lanes(a_log[l]), d_skip[l].astype(F32)[head_of_lane].reshape(1, -1), expand,
                   g_ssd_out[l].reshape(1, -1))
        x1, h2 = _outproj(att, ssd, x, mods, g_ffn[l].reshape(1, d), w_o)
        g_last = g_final.reshape(1, d) if l == depth - 1 else None
        x = _ffn(h2, x1, mods, g_last, w_g, w_u, w_d)
    return x
```

```python
import functools

import jax
import jax.numpy as jnp
from jax import lax
from jax.experimental import pallas as pl
from jax.experimental.pallas import tpu as pltpu

F32 = jnp.float32
BF16 = jnp.bfloat16

D_MODEL = 2048
CHUNK = 64
LEFT_CHUNKS = 8
BAND = (LEFT_CHUNKS + 1) * CHUNK
ATT_HEADS = 16
ATT_HEAD_DIM = 64
ATT_WIDTH = ATT_HEADS * ATT_HEAD_DIM
REL_CLIP = 256
REL_FUTURE = CHUNK - 1
SSD_HEADS = 16
SSD_HEAD_DIM = 64
SSD_WIDTH = SSD_HEADS * SSD_HEAD_DIM
SSD_GROUPS = 2
SSD_STATE = 128
SSD_CONV = 4
XBC_WIDTH = SSD_WIDTH + 2 * SSD_GROUPS * SSD_STATE
GROUP_WIDTH = SSD_WIDTH // SSD_GROUPS
FFN_HIDDEN = 5632
N_MOD = 6
EPS = 1e-6

LANES = 128
BF16_SUBLANES = 16
MXU_WIDTH = 256
ATT_GROUP_WIDTH = MXU_WIDTH
ATT_GROUP_HEADS = ATT_GROUP_WIDTH // ATT_HEAD_DIM
ATT_GROUPS = ATT_WIDTH // ATT_GROUP_WIDTH
BAND_PAD = -(-BAND // LANES) * LANES
VMEM_LIMIT = 60 * 1024 * 1024

ADA_TN = 1024
PROJ_TM = 512
PROJ_TN = 512
ATT_TQ = 1024
ATT_LEFT = LEFT_CHUNKS * CHUNK
SSD_TB = 512
SSD_L = 128
OUT_TM = 512
OUT_SUB = 256
FFN_TM = 1024
FFN_TH = 512
FFN_TN = 512

HIGHEST = lax.Precision.HIGHEST


def _params(*sem):
    return pltpu.CompilerParams(dimension_semantics=sem, vmem_limit_bytes=VMEM_LIMIT)


def _resident(shape):
    nd = len(shape)
    return pl.BlockSpec(shape, lambda *_: (0,) * nd, pipeline_mode=pl.Buffered(1))


def _rms(x):
    return x * lax.rsqrt(jnp.mean(x * x, axis=-1, keepdims=True) + EPS)


def _silu(x):
    h = 0.5 * x
    return h + h * jnp.tanh(h)


def _split_bf16(x, terms):
    parts = []
    for _ in range(terms):
        p = x.astype(BF16)
        parts.append(p)
        x = x - p.astype(F32)
    return parts


def _adaln_kernel(c_ref, w_ref, b_ref, o_ref):
    cond = _silu(c_ref[...])
    o_ref[...] = jnp.dot(cond.astype(BF16), w_ref[...].astype(BF16),
                         preferred_element_type=F32) + b_ref[...]


def _adaln(c_pad, w, b):
    rows, d = c_pad.shape
    n = w.shape[1]
    return pl.pallas_call(
        _adaln_kernel,
        grid=(n // ADA_TN,),
        in_specs=[pl.BlockSpec((rows, d), lambda j: (0, 0)),
                  pl.BlockSpec((d, ADA_TN), lambda j: (0, j)),
                  pl.BlockSpec((1, ADA_TN), lambda j: (0, j))],
        out_specs=pl.BlockSpec((rows, ADA_TN), lambda j: (0, j)),
        out_shape=jax.ShapeDtypeStruct((rows, n), F32),
        compiler_params=_params("arbitrary"),
        name="adaln",
    )(c_pad, w, b)


_PROJ_SPLITS = (ATT_WIDTH, ATT_WIDTH, ATT_WIDTH, SSD_WIDTH, XBC_WIDTH)
_NT = (((1,), (1,)), ((), ()))


def _cast_rows(n_rows, n_steps):
    rows = next(r for r in range(BF16_SUBLANES, n_rows + 1, BF16_SUBLANES)
                if n_rows % r == 0 and r * n_steps >= n_rows)
    return rows


def _inproj_kernel(n_cast, x_ref, mods_ref, g_ref, w_ref, wdt_ref, *refs):
    cast_in, refs = refs[:n_cast], refs[n_cast:]
    (q_ref, k_ref, v_ref, z_ref, xbc_ref, dt_ref), refs = refs[:6], refs[6:]
    cast_out, (h_s,) = refs[:n_cast], refs[n_cast:]
    for src, dst in zip(cast_in, cast_out):
        dst[...] = src[...].astype(dst.dtype)
    x = x_ref[...]
    h = _rms(x) * g_ref[...]
    h = h * (1.0 + mods_ref[1:2, :]) + mods_ref[0:1, :]
    h_s[...] = h.astype(BF16)
    col = 0
    for ref, width in zip((q_ref, k_ref, v_ref, z_ref, xbc_ref), _PROJ_SPLITS):
        for j in range(0, width, PROJ_TN):
            ref[:, j:j + PROJ_TN] = lax.dot_general(
                h_s[...], w_ref[col + j:col + j + PROJ_TN, :], _NT,
                preferred_element_type=F32).astype(ref.dtype)
        col += width
    dt_ref[...] = lax.dot_general(h_s[...], wdt_ref[...], _NT, preferred_element_type=F32)


def _inproj(x, mods, g, w, w_dt, to_cast):
    b, s, d = x.shape
    tm = PROJ_TM
    steps_per_seq = s // tm
    row = lambda bi, i: (bi, i, 0)
    out_shape = [jax.ShapeDtypeStruct((b, s, w), BF16) for w in _PROJ_SPLITS]
    out_shape.append(jax.ShapeDtypeStruct((b, s, LANES), F32))
    out_specs = [pl.BlockSpec((None, tm, w), row) for w in _PROJ_SPLITS]
    out_specs.append(pl.BlockSpec((None, tm, LANES), row))
    cast_specs = []
    for a in to_cast:
        rows = _cast_rows(a.shape[0], b * steps_per_seq)
        blk = functools.partial(lambda n_blk, bi, i: (jnp.minimum(bi * steps_per_seq + i, n_blk - 1), 0),
                                a.shape[0] // rows)
        cast_specs.append(pl.BlockSpec((rows, a.shape[1]), blk))
        out_shape.append(jax.ShapeDtypeStruct(a.shape, BF16))
    return pl.pallas_call(
        functools.partial(_inproj_kernel, len(to_cast)),
        grid=(b, steps_per_seq),
        in_specs=[pl.BlockSpec((None, tm, d), row),
                  pl.BlockSpec((None, N_MOD, d), lambda bi, i: (bi, 0, 0)),
                  _resident(g.shape), _resident(w.shape), _resident(w_dt.shape)] + cast_specs,
        out_specs=out_specs + cast_specs,
        out_shape=out_shape,
        scratch_shapes=[pltpu.VMEM((tm, d), BF16)],
        compiler_params=_params("arbitrary", "arbitrary"),
        name="inproj",
    )(x, mods, g, w, w_dt, *to_cast)


LOG2E = 1.4426950408889634


def _attn_kernel(q_ref, kp_ref, kc_ref, vp_ref, vc_ref, bias_ref, g_ref, o_ref,
                 k_s, v_s, s_s, pt_s, max_s, den_s, o_s, inv_s):
    i = pl.program_id(1)
    tq = ATT_TQ
    k_s[0:ATT_LEFT, :] = kp_ref[...]
    k_s[ATT_LEFT:, :] = kc_ref[...]
    v_s[0:ATT_LEFT, :] = vp_ref[...]
    v_s[ATT_LEFT:ATT_LEFT + tq, :] = vc_ref[...]
    v_s[ATT_LEFT + tq:, :] = jnp.zeros((BAND_PAD - BAND, ATT_WIDTH), BF16)

    gw = ATT_GROUP_WIDTH
    rr = lax.broadcasted_iota(jnp.int32, (gw, gw), 0) // CHUNK
    ll = lax.broadcasted_iota(jnp.int32, (gw, gw), 1) // ATT_HEAD_DIM
    own_head = rr == ll
    lane_head = lax.broadcasted_iota(jnp.int32, (CHUNK, gw), 1) // ATT_HEAD_DIM
    groups = range(ATT_GROUPS)
    n_chunks = tq // CHUNK

    def scores_stage(c, slot):
        q0 = c * CHUNK if isinstance(c, int) else pl.multiple_of(c * CHUNK, CHUNK)
        for g in groups:
            lanes = slice(g * gw, (g + 1) * gw)
            qg = q_ref[pl.ds(q0, CHUNK), lanes]
            q_bd = jnp.where(own_head, jnp.concatenate([qg] * ATT_GROUP_HEADS, axis=0), jnp.zeros((), BF16))
            kb = k_s[pl.ds(q0, BAND), lanes]
            st = lax.dot_general(kb, q_bd, (((1,), (1,)), ((), ())),
                                 preferred_element_type=F32) + bias_ref[g]
            s_s[slot, g] = st
            max_s[slot, g] = jnp.broadcast_to(jnp.max(st, axis=0, keepdims=True), (8, gw))

    def mask_stage(c, slot):
        n_before = jnp.maximum(LEFT_CHUNKS - (i * n_chunks + c), 0)

        def mask_slab(j, carry):
            r0 = pl.multiple_of(j * CHUNK, CHUNK)
            for g in groups:
                s_s[slot, g, pl.ds(r0, CHUNK), :] = jnp.full((CHUNK, gw), -jnp.inf, F32)
            return carry

        lax.fori_loop(0, n_before, mask_slab, 0)

        @pl.when(n_before > 0)
        def _():
            for g in groups:
                max_s[slot, g] = jnp.broadcast_to(jnp.max(s_s[slot, g], axis=0, keepdims=True), (8, gw))

    def softmax_stage(slot):
        for g in groups:
            e = jnp.exp2(s_s[slot, g] - max_s[slot, g, 0:1, :])
            den_s[slot, g] = jnp.broadcast_to(jnp.sum(e, axis=0, keepdims=True), (8, gw))
            eb = jnp.concatenate([e.astype(BF16), jnp.zeros((BAND_PAD - BAND, gw), BF16)], axis=0)
            for r in range(0, BAND_PAD, LANES):
                pt_s[slot, g, :, r:r + LANES] = eb[r:r + LANES, :].T

    def pv_stage(c, slot):
        q0 = c * CHUNK if isinstance(c, int) else pl.multiple_of(c * CHUNK, CHUNK)
        for g in groups:
            lanes = slice(g * gw, (g + 1) * gw)
            vb = v_s[pl.ds(q0, BAND_PAD), lanes]
            o4 = jnp.dot(pt_s[slot, g], vb, preferred_element_type=F32)
            out = o4[0:CHUNK]
            for hj in range(1, ATT_GROUP_HEADS):
                out = jnp.where(lane_head == hj, o4[hj * CHUNK:(hj + 1) * CHUNK], out)
            o_s[slot, :, lanes] = out
            inv_s[slot, g] = 1.0 / den_s[slot, g]

    def finalize_stage(c, slot):
        q0 = c * CHUNK if isinstance(c, int) else pl.multiple_of(c * CHUNK, CHUNK)
        scales = []
        for g in groups:
            inv_den = inv_s[slot, g, 0:1, :]
            scale = None
            for half in range(gw // LANES):
                col = jnp.broadcast_to(inv_den[:, half * LANES:(half + 1) * LANES], (LANES, LANES)).T
                for j in range(LANES // CHUNK):
                    hj = half * (LANES // CHUNK) + j
                    blk = jnp.concatenate([col[j * CHUNK:(j + 1) * CHUNK]] * (gw // LANES), axis=1)
                    scale = blk if scale is None else jnp.where(lane_head == hj, blk, scale)
            scales.append(scale)
        att = o_s[slot] * jnp.concatenate(scales, axis=-1)
        o_ref[pl.ds(q0, CHUNK), :] = (_rms(att) * g_ref[...]).astype(o_ref.dtype)

    def block(c, slot, first=0, last=n_chunks - 1):
        if first <= c <= last:
            scores_stage(c, slot)
        if first <= c - 1 <= last:
            softmax_stage(1 - slot)
        if first <= c - 2 <= last:
            pv_stage(c - 2, slot)
        if first <= c - 3 <= last:
            finalize_stage(c - 3, 1 - slot)
        if first <= c <= last:
            mask_stage(c, slot)

    def steady_pair(j, carry):
        c = 3 + 2 * j
        for k in range(2):
            scores_stage(c + k, (1 + k) % 2)
            softmax_stage(k % 2)
            pv_stage(c + k - 2, (1 + k) % 2)
            finalize_stage(c + k - 3, k % 2)
            mask_stage(c + k, (1 + k) % 2)
        return carry

    n_pairs = (n_chunks - 4) // 2
    assert n_chunks >= 4 and n_chunks % 2 == 0
    for c in range(3):
        block(c, c % 2)
    lax.fori_loop(0, n_pairs, steady_pair, 0)
    for c in range(3 + 2 * n_pairs, n_chunks + 3):
        block(c, c % 2)


def _attention(q, k, v, bias_t, g):
    b, s, w = q.shape
    tq = ATT_TQ
    assert tq % ATT_LEFT == 0
    cur = lambda bi, i: (bi, i, 0)
    prev = lambda bi, i: (bi, jnp.maximum(i * (tq // ATT_LEFT) - 1, 0), 0)
    blk = (None, tq, w)
    left = (None, ATT_LEFT, w)
    return pl.pallas_call(
        _attn_kernel,
        grid=(b, s // tq),
        in_specs=[pl.BlockSpec(blk, cur),
                  pl.BlockSpec(left, prev), pl.BlockSpec(blk, cur),
                  pl.BlockSpec(left, prev), pl.BlockSpec(blk, cur),
                  _resident(bias_t.shape), _resident(g.shape)],
        out_specs=pl.BlockSpec(blk, cur),
        out_shape=jax.ShapeDtypeStruct((b, s, w), BF16),
        scratch_shapes=[pltpu.VMEM((ATT_LEFT + tq, w), BF16),
                        pltpu.VMEM((ATT_LEFT + tq + BAND_PAD - BAND, w), BF16),
                        pltpu.VMEM((2, ATT_GROUPS, BAND, ATT_GROUP_WIDTH), F32),
                        pltpu.VMEM((2, ATT_GROUPS, ATT_GROUP_WIDTH, BAND_PAD), BF16),
                        pltpu.VMEM((2, ATT_GROUPS, 8, ATT_GROUP_WIDTH), F32),
                        pltpu.VMEM((2, ATT_GROUPS, 8, ATT_GROUP_WIDTH), F32),
                        pltpu.VMEM((2, CHUNK, w), F32),
                        pltpu.VMEM((2, ATT_GROUPS, 8, ATT_GROUP_WIDTH), F32)],
        compiler_params=_params("arbitrary", "arbitrary"),
        name="chunk_attn",
    )(q, k, k, v, v, bias_t, g)


def _bias_table(rel_bias):
    n_rel = rel_bias.shape[1]
    n_f = BAND + REL_FUTURE
    f = jnp.concatenate([jnp.broadcast_to(rel_bias[:, n_rel - 1:], (ATT_HEADS, n_f - n_rel)),
                         rel_bias[:, ::-1]], axis=1).astype(F32) * LOG2E
    skew = jnp.broadcast_to(jnp.pad(f, ((0, 0), (0, 1)))[:, None, :], (ATT_HEADS, CHUNK, n_f + 1))
    skew = skew.reshape(ATT_HEADS, CHUNK * (n_f + 1))[:, :CHUNK * n_f].reshape(ATT_HEADS, CHUNK, n_f)
    bias = skew[:, :, REL_FUTURE:]
    bias = bias.reshape(ATT_GROUPS, ATT_GROUP_HEADS, CHUNK, BAND)
    return jnp.transpose(bias, (0, 3, 1, 2)).reshape(ATT_GROUPS, BAND, ATT_GROUP_WIDTH)


_HALO = 8


def _ssd_kernel(xbc_ref, z_ref, dt_ref, convw_ref, convb_ref, dtb_ref, a_ref, dskip_ref,
                expand_ref, g_ref, o_ref, raw_s, act_s, dt_s, y_s, h_s):
    i = pl.program_id(1)
    tb = SSD_TB
    L = SSD_L

    @pl.when(i == 0)
    def _():
        raw_s[0:_HALO, :] = jnp.zeros((_HALO, XBC_WIDTH), F32)
        h_s[...] = jnp.zeros_like(h_s)

    @pl.when(i > 0)
    def _():
        raw_s[0:_HALO, :] = raw_s[tb:tb + _HALO, :]

    raw_s[_HALO:, :] = xbc_ref[...].astype(F32)
    assert SSD_CONV == 4
    raw = raw_s[...]
    back1 = pltpu.roll(raw, 1, axis=0)
    near = convw_ref[3:4, :] * raw + convw_ref[2:3, :] * back1
    far = convw_ref[1:2, :] * raw + convw_ref[0:1, :] * back1
    conv = convb_ref[...] + near[_HALO:] + pltpu.roll(far, 2, axis=0)[_HALO:]
    act_s[...] = _silu(conv)

    dtr = dt_ref[...] + dtb_ref[...]
    dt_s[...] = jnp.maximum(dtr, 0.0) + jnp.log1p(jnp.exp(-jnp.abs(dtr)))

    a = -jnp.exp(a_ref[...]) * LOG2E
    ri = lax.broadcasted_iota(jnp.int32, (L, L), 0)
    ci = lax.broadcasted_iota(jnp.int32, (L, L), 1)
    causal = ri >= ci
    tril = causal.astype(BF16)
    low_lanes = lax.broadcasted_iota(jnp.int32, (L, LANES), 1) < SSD_HEAD_DIM
    expand = expand_ref[...]
    dskip = dskip_ref[...]
    bc0 = SSD_WIDTH
    cc0 = SSD_WIDTH + SSD_GROUPS * SSD_STATE

    def chunk_body(c, carry):
        r0 = pl.multiple_of(c * L, L)
        dtc = dt_s[pl.ds(r0, L), :]
        adt = dtc * a
        cs3 = jnp.dot(tril, jnp.concatenate(_split_bf16(adt, 3), axis=1), preferred_element_type=F32)
        cs = cs3[:, 0:LANES] + cs3[:, LANES:2 * LANES] + cs3[:, 2 * LANES:]
        cs_t = cs.T
        last = cs[L - 1:L, :]
        stacked = jnp.concatenate([dtc, jnp.exp2(last - cs), jnp.exp2(cs)], axis=0)
        wide = jnp.dot(jnp.concatenate(_split_bf16(stacked, 2), axis=1), expand,
                       preferred_element_type=F32)
        dt_w, decay_w, ecs_w = wide[0:L], wide[L:2 * L], wide[2 * L:3 * L]
        cdec = jnp.broadcast_to(jnp.exp2(last), (8, LANES))
        cdec_w = sum(jnp.dot(part, expand[:LANES], preferred_element_type=F32)
                     for part in _split_bf16(cdec, 3))[0:1]

        xs = act_s[pl.ds(r0, L), 0:SSD_WIDTH]
        xdt = xs * dt_w
        xdt_b = xdt.astype(BF16)
        xdec_b = (xdt * decay_w).astype(BF16)
        ys = []
        for g in range(SSD_GROUPS):
            gl = slice(g * GROUP_WIDTH, (g + 1) * GROUP_WIDTH)
            bm = act_s[pl.ds(r0, L), bc0 + g * SSD_STATE:bc0 + (g + 1) * SSD_STATE].astype(BF16)
            cm = act_s[pl.ds(r0, L), cc0 + g * SSD_STATE:cc0 + (g + 1) * SSD_STATE].astype(BF16)
            cb = lax.dot_general(cm, bm, (((1,), (1,)), ((), ())), preferred_element_type=F32)
            hprev = h_s[g]
            y_off = jnp.dot(cm, hprev.astype(BF16), preferred_element_type=F32) * ecs_w[:, gl]
            states = lax.dot_general(bm, xdec_b[:, gl], (((0,), (0,)), ((), ())),
                                     preferred_element_type=F32)
            h_s[g] = hprev * cdec_w[:, gl] + states
            yd = []
            for pr in range(GROUP_WIDTH // LANES):
                hp = g * (GROUP_WIDTH // LANES) + pr
                xp = xdt_b[:, hp * LANES:(hp + 1) * LANES]
                halves = []
                for hh in (2 * hp, 2 * hp + 1):
                    seg = jnp.exp2(jnp.where(causal, cs[:, hh:hh + 1] - cs_t[hh:hh + 1, :], -jnp.inf))
                    halves.append(jnp.dot((cb * seg).astype(BF16), xp, preferred_element_type=F32))
                yd.append(jnp.where(low_lanes, halves[0], halves[1]))
            ys.append(jnp.concatenate(yd, axis=-1) + y_off)
        y = jnp.concatenate(ys, axis=-1) + xs * dskip
        y_s[pl.ds(r0, L), :] = y
        return carry

    lax.fori_loop(0, tb // L, chunk_body, 0, unroll=True)
    gated = y_s[...] * _silu(z_ref[...].astype(F32))
    o_ref[...] = (_rms(gated) * g_ref[...]).astype(o_ref.dtype)


def _ssd(xbc, z, dt_raw, conv_w, conv_b, dt_bias, a_log, dskip_w, expand, g):
    b, s, _ = xbc.shape
    tb = SSD_TB
    row = lambda bi, i: (bi, i, 0)
    return pl.pallas_call(
        _ssd_kernel,
        grid=(b, s // tb),
        in_specs=[pl.BlockSpec((None, tb, XBC_WIDTH), row),
                  pl.BlockSpec((None, tb, SSD_WIDTH), row),
                  pl.BlockSpec((None, tb, LANES), row),
                  _resident(conv_w.shape), _resident(conv_b.shape), _resident(dt_bias.shape),
                  _resident(a_log.shape), _resident(dskip_w.shape), _resident(expand.shape),
                  _resident(g.shape)],
        out_specs=pl.BlockSpec((None, tb, SSD_WIDTH), row),
        out_shape=jax.ShapeDtypeStruct((b, s, SSD_WIDTH), BF16),
        scratch_shapes=[pltpu.VMEM((_HALO + tb, XBC_WIDTH), F32),
                        pltpu.VMEM((tb, XBC_WIDTH), F32),
                        pltpu.VMEM((tb, LANES), F32),
                        pltpu.VMEM((tb, SSD_WIDTH), F32),
                        pltpu.VMEM((SSD_GROUPS, SSD_STATE, GROUP_WIDTH), F32)],
        compiler_params=_params("arbitrary", "arbitrary"),
        name="ssd",
    )(xbc, z, dt_raw, conv_w, conv_b, dt_bias, a_log, dskip_w, expand, g)


def _outproj_kernel(att_ref, ssd_ref, x_ref, mods_ref, g_ref, w_ref, x1_ref, h_ref):
    for r0 in range(0, x_ref.shape[0], OUT_SUB):
        rows = slice(r0, r0 + OUT_SUB)
        mix = jnp.dot(att_ref[rows, :], w_ref[0:ATT_WIDTH, :], preferred_element_type=F32)
        mix = mix + jnp.dot(ssd_ref[rows, :], w_ref[ATT_WIDTH:, :], preferred_element_type=F32)
        x1 = x_ref[rows, :] + mods_ref[2:3, :] * mix
        x1_ref[rows, :] = x1
        h = _rms(x1) * g_ref[...]
        h_ref[rows, :] = (h * (1.0 + mods_ref[4:5, :]) + mods_ref[3:4, :]).astype(h_ref.dtype)


def _outproj(att, ssd, x, mods, g, w):
    b, s, d = x.shape
    tm = OUT_TM
    row = lambda bi, i: (bi, i, 0)
    return pl.pallas_call(
        _outproj_kernel,
        grid=(b, s // tm),
        in_specs=[pl.BlockSpec((None, tm, ATT_WIDTH), row),
                  pl.BlockSpec((None, tm, SSD_WIDTH), row),
                  pl.BlockSpec((None, tm, d), row),
                  pl.BlockSpec((None, N_MOD, d), lambda bi, i: (bi, 0, 0)),
                  _resident(g.shape), _resident(w.shape)],
        out_specs=[pl.BlockSpec((None, tm, d), row), pl.BlockSpec((None, tm, d), row)],
        out_shape=[jax.ShapeDtypeStruct((b, s, d), F32), jax.ShapeDtypeStruct((b, s, d), BF16)],
        compiler_params=_params("arbitrary", "arbitrary"),
        name="outproj",
    )(att, ssd, x, mods, g, w)


def _ffn_kernel(final_norm, h_ref, x1_ref, mods_ref, g_ref, wg_ref, wu_ref, wd_ref, o_ref):
    k = pl.program_id(2)

    def hidden_block(base_ref):
        h = h_ref[...]
        gate = jnp.dot(h, wg_ref[...], preferred_element_type=F32)
        up = jnp.dot(h, wu_ref[...], preferred_element_type=F32)
        act = (_silu(gate) * up).astype(BF16)
        for n0 in range(0, o_ref.shape[-1], FFN_TN):
            cols = slice(n0, n0 + FFN_TN)
            o_ref[:, cols] = base_ref[:, cols] + mods_ref[5:6, cols] * jnp.dot(
                act, wd_ref[:, cols], preferred_element_type=F32)

    pl.when(k == 0)(functools.partial(hidden_block, x1_ref))
    pl.when(k > 0)(functools.partial(hidden_block, o_ref))

    if final_norm:
        @pl.when(k == pl.num_programs(2) - 1)
        def _():
            o_ref[...] = _rms(o_ref[...]) * g_ref[...]


def _ffn(h, x1, mods, g_final, w_gate, w_up, w_down):
    b, s, d = x1.shape
    tm, th = FFN_TM, FFN_TH
    hid = w_gate.shape[1]
    row = lambda bi, i, k: (bi, i, 0)
    final_norm = g_final is not None
    g = g_final if final_norm else jnp.ones((1, d), F32)
    return pl.pallas_call(
        functools.partial(_ffn_kernel, final_norm),
        grid=(b, s // tm, hid // th),
        in_specs=[pl.BlockSpec((None, tm, d), row),
                  pl.BlockSpec((None, tm, d), row),
                  pl.BlockSpec((None, N_MOD, d), lambda bi, i, k: (bi, 0, 0)),
                  pl.BlockSpec(g.shape, lambda bi, i, k: (0, 0)),
                  pl.BlockSpec((d, th), lambda bi, i, k: (0, k)),
                  pl.BlockSpec((d, th), lambda bi, i, k: (0, k)),
                  pl.BlockSpec((th, d), lambda bi, i, k: (k, 0))],
        out_specs=pl.BlockSpec((None, tm, d), row),
        out_shape=jax.ShapeDtypeStruct((b, s, d), F32),
        compiler_params=_params("arbitrary", "arbitrary", "arbitrary"),
        name="ffn",
    )(h, x1, mods, g, w_gate, w_up, w_down)


def _pad_lanes(v):
    return jnp.pad(v.astype(F32), (0, LANES - v.shape[0])).reshape(1, LANES)


def kernel(x, c, w_ada, b_ada, g_mix, w_in, rel_bias, conv_w, conv_b, dt_bias, a_log, d_skip,
           g_att_out, g_ssd_out, w_out, g_ffn, w_gate, w_up, w_down, g_final):
    b, s, d = x.shape
    depth = w_ada.shape[0]
    n_main = sum(_PROJ_SPLITS)
    head_of_lane = jnp.arange(SSD_WIDTH) // SSD_HEAD_DIM
    expand = (jnp.arange(2 * LANES)[:, None] % LANES == head_of_lane[None, :]).astype(BF16)
    c_pad = jnp.pad(c, ((0, 8 - b % 8 if b % 8 else 0), (0, 0)))
    for l in range(depth):
        mods = _adaln(c_pad, w_ada[l], b_ada[l].reshape(1, -1))[:b].reshape(b, N_MOD, d)
        n_in = w_in.shape[-1]
        q_scale = jnp.where(jnp.arange(n_in) < ATT_WIDTH, ATT_HEAD_DIM ** -0.5 * LOG2E, 1.0).astype(F32)
        w_t = w_in[l].T
        w_proj = (w_t * q_scale[:, None]).astype(BF16)
        w_dt = jnp.pad(w_proj[n_main:], ((0, LANES - (n_in - n_main)), (0, 0)))
        q, k, v, z, xbc, dt_raw, w_o, w_g, w_u, w_d = _inproj(
            x, mods, g_mix[l].reshape(1, d), w_proj, w_dt, (w_out[l], w_gate[l], w_up[l], w_down[l]))
        att = _attention(q, k, v, _bias_table(rel_bias[l]), g_att_out[l].reshape(1, -1))
        ssd = _ssd(xbc, z, dt_raw, conv_w[l], conv_b[l].reshape(1, -1), _pad_lanes(dt_bias[l]),
                   _pad_lanes(a_log[l]), d_skip[l].astype(F32)[head_of_lane].reshape(1, -1), expand,
                   g_ssd_out[l].reshape(1, -1))
        x1, h2 = _outproj(att, ssd, x, mods, g_ffn[l].reshape(1, d), w_o)
        g_last = g_final.reshape(1, d) if l == depth - 1 else None
        x = _ffn(h2, x1, mods, g_last, w_g, w_u, w_d)
    return x
```

```python
import functools

import jax
import jax.numpy as jnp
import numpy as np
from jax import lax
from jax.experimental import pallas as pl
from jax.experimental.pallas import tpu as pltpu

F32 = jnp.float32
BF16 = jnp.bfloat16

D_MODEL = 2048
CHUNK = 64
LEFT_CHUNKS = 8
BAND = (LEFT_CHUNKS + 1) * CHUNK
ATT_HEADS = 16
ATT_HEAD_DIM = 64
ATT_WIDTH = ATT_HEADS * ATT_HEAD_DIM
REL_CLIP = 256
REL_FUTURE = CHUNK - 1
SSD_HEADS = 16
SSD_HEAD_DIM = 64
SSD_WIDTH = SSD_HEADS * SSD_HEAD_DIM
SSD_GROUPS = 2
SSD_STATE = 128
SSD_CONV = 4
XBC_WIDTH = SSD_WIDTH + 2 * SSD_GROUPS * SSD_STATE
GROUP_WIDTH = SSD_WIDTH // SSD_GROUPS
FFN_HIDDEN = 5632
N_MOD = 6
EPS = 1e-6

LANES = 128
BF16_SUBLANES = 16
MXU_WIDTH = 256
ATT_GROUP_WIDTH = MXU_WIDTH
ATT_GROUP_HEADS = ATT_GROUP_WIDTH // ATT_HEAD_DIM
ATT_GROUPS = ATT_WIDTH // ATT_GROUP_WIDTH
BAND_PAD = -(-BAND // LANES) * LANES
VMEM_LIMIT = 60 * 1024 * 1024

ADA_TN = 1024
PROJ_TM = 512
PROJ_TN = 512
ATT_TQ = 1024
ATT_LEFT = LEFT_CHUNKS * CHUNK
SSD_TB = 512
SSD_L = 128
OUT_TM = 512
OUT_SUB = 256
FFN_TM = 1024
FFN_TH = 512
FFN_TN = 512

HIGHEST = lax.Precision.HIGHEST


def _params(*sem):
    return pltpu.CompilerParams(dimension_semantics=sem, vmem_limit_bytes=VMEM_LIMIT)


def _resident(shape):
    nd = len(shape)
    return pl.BlockSpec(shape, lambda *_: (0,) * nd, pipeline_mode=pl.Buffered(1))


def _rms(x):
    return x * lax.rsqrt(jnp.mean(x * x, axis=-1, keepdims=True) + EPS)


def _silu(x):
    h = 0.5 * x
    return h + h * jnp.tanh(h)


def _split_bf16(x, terms):
    parts = []
    for _ in range(terms):
        p = x.astype(BF16)
        parts.append(p)
        x = x - p.astype(F32)
    return parts


def _adaln_kernel(c_ref, w_ref, b_ref, o_ref):
    cond = _silu(c_ref[...])
    o_ref[...] = jnp.dot(cond.astype(BF16), w_ref[...].astype(BF16),
                         preferred_element_type=F32) + b_ref[...]


def _adaln(c_pad, w, b):
    rows, d = c_pad.shape
    n = w.shape[1]
    return pl.pallas_call(
        _adaln_kernel,
        grid=(n // ADA_TN,),
        in_specs=[pl.BlockSpec((rows, d), lambda j: (0, 0)),
                  pl.BlockSpec((d, ADA_TN), lambda j: (0, j)),
                  pl.BlockSpec((1, ADA_TN), lambda j: (0, j))],
        out_specs=pl.BlockSpec((rows, ADA_TN), lambda j: (0, j)),
        out_shape=jax.ShapeDtypeStruct((rows, n), F32),
        compiler_params=_params("arbitrary"),
        name="adaln",
    )(c_pad, w, b)


_PROJ_SPLITS = (ATT_WIDTH, ATT_WIDTH, ATT_WIDTH, SSD_WIDTH, XBC_WIDTH)
_NT = (((1,), (1,)), ((), ()))


def _cast_rows(n_rows, n_steps):
    rows = next(r for r in range(BF16_SUBLANES, n_rows + 1, BF16_SUBLANES)
                if n_rows % r == 0 and r * n_steps >= n_rows)
    return rows


def _inproj_kernel(n_cast, x_ref, mods_ref, g_ref, w_ref, wdt_ref, *refs):
    cast_in, refs = refs[:n_cast], refs[n_cast:]
    (q_ref, k_ref, v_ref, z_ref, xbc_ref, dt_ref), refs = refs[:6], refs[6:]
    cast_out, (h_s,) = refs[:n_cast], refs[n_cast:]
    for src, dst in zip(cast_in, cast_out):
        dst[...] = src[...].astype(dst.dtype)
    x = x_ref[...]
    h = _rms(x) * g_ref[...]
    h = h * (1.0 + mods_ref[1:2, :]) + mods_ref[0:1, :]
    h_s[...] = h.astype(BF16)
    col = 0
    for ref, width in zip((q_ref, k_ref, v_ref, z_ref, xbc_ref), _PROJ_SPLITS):
        for j in range(0, width, PROJ_TN):
            ref[:, j:j + PROJ_TN] = lax.dot_general(
                h_s[...], w_ref[col + j:col + j + PROJ_TN, :], _NT,
                preferred_element_type=F32).astype(ref.dtype)
        col += width
    dt_ref[...] = lax.dot_general(h_s[...], wdt_ref[...], _NT, preferred_element_type=F32)


def _inproj(x, mods, g, w, w_dt, to_cast):
    b, s, d = x.shape
    tm = PROJ_TM
    steps_per_seq = s // tm
    row = lambda bi, i: (bi, i, 0)
    out_shape = [jax.ShapeDtypeStruct((b, s, w), BF16) for w in _PROJ_SPLITS]
    out_shape.append(jax.ShapeDtypeStruct((b, s, LANES), F32))
    out_specs = [pl.BlockSpec((None, tm, w), row) for w in _PROJ_SPLITS]
    out_specs.append(pl.BlockSpec((None, tm, LANES), row))
    cast_specs = []
    for a in to_cast:
        rows = _cast_rows(a.shape[0], b * steps_per_seq)
        blk = functools.partial(lambda n_blk, bi, i: (jnp.minimum(bi * steps_per_seq + i, n_blk - 1), 0),
                                a.shape[0] // rows)
        cast_specs.append(pl.BlockSpec((rows, a.shape[1]), blk))
        out_shape.append(jax.ShapeDtypeStruct(a.shape, BF16))
    return pl.pallas_call(
        functools.partial(_inproj_kernel, len(to_cast)),
        grid=(b, steps_per_seq),
        in_specs=[pl.BlockSpec((None, tm, d), row),
                  pl.BlockSpec((None, N_MOD, d), lambda bi, i: (bi, 0, 0)),
                  _resident(g.shape), _resident(w.shape), _resident(w_dt.shape)] + cast_specs,
        out_specs=out_specs + cast_specs,
        out_shape=out_shape,
        scratch_shapes=[pltpu.VMEM((tm, d), BF16)],
        compiler_params=_params("arbitrary", "arbitrary"),
        name="inproj",
    )(x, mods, g, w, w_dt, *to_cast)


LOG2E = 1.4426950408889634


def _attn_kernel(q_ref, kp_ref, kc_ref, vp_ref, vc_ref, bias_ref, g_ref, o_ref,
                 k_s, v_s, s_s, pt_s, max_s, o_s):
    i = pl.program_id(1)
    tq = ATT_TQ
    k_s[0:ATT_LEFT, :] = kp_ref[...]
    k_s[ATT_LEFT:, :] = kc_ref[...]
    v_s[0:ATT_LEFT, :] = vp_ref[...]
    v_s[ATT_LEFT:ATT_LEFT + tq, :] = vc_ref[...]
    v_s[ATT_LEFT + tq:, :] = jnp.zeros((BAND_PAD - BAND, ATT_WIDTH), BF16)

    gw = ATT_GROUP_WIDTH
    rr = lax.broadcasted_iota(jnp.int32, (gw, gw), 0) // CHUNK
    ll = lax.broadcasted_iota(jnp.int32, (gw, gw), 1) // ATT_HEAD_DIM
    own_head = rr == ll
    lane_head = lax.broadcasted_iota(jnp.int32, (CHUNK, gw), 1) // ATT_HEAD_DIM
    groups = range(ATT_GROUPS)
    n_chunks = tq // CHUNK

    def scores_stage(c, slot):
        q0 = c * CHUNK if isinstance(c, int) else pl.multiple_of(c * CHUNK, CHUNK)
        for g in groups:
            lanes = slice(g * gw, (g + 1) * gw)
            qg = q_ref[pl.ds(q0, CHUNK), lanes]
            q_bd = jnp.where(own_head, jnp.concatenate([qg] * ATT_GROUP_HEADS, axis=0), jnp.zeros((), BF16))
            kb = k_s[pl.ds(q0, BAND), lanes]
            st = lax.dot_general(kb, q_bd, (((1,), (1,)), ((), ())),
                                 preferred_element_type=F32) + bias_ref[g]
            s_s[slot, g] = st
            max_s[slot, g] = jnp.broadcast_to(jnp.max(st, axis=0, keepdims=True), (8, gw))

    def mask_stage(c, slot):
        n_before = jnp.maximum(LEFT_CHUNKS - (i * n_chunks + c), 0)

        def mask_slab(j, carry):
            r0 = pl.multiple_of(j * CHUNK, CHUNK)
            for g in groups:
                s_s[slot, g, pl.ds(r0, CHUNK), :] = jnp.full((CHUNK, gw), -jnp.inf, F32)
            return carry

        lax.fori_loop(0, n_before, mask_slab, 0)

        @pl.when(n_before > 0)
        def _():
            for g in groups:
                max_s[slot, g] = jnp.broadcast_to(jnp.max(s_s[slot, g], axis=0, keepdims=True), (8, gw))

    def softmax_stage(slot):
        for g in groups:
            e = jnp.exp2(s_s[slot, g] - max_s[slot, g, 0:1, :])
            probs = e * (1.0 / jnp.sum(e, axis=0, keepdims=True))
            eb = jnp.concatenate([probs.astype(BF16), jnp.zeros((BAND_PAD - BAND, gw), BF16)], axis=0)
            for r in range(0, BAND_PAD, LANES):
                pt_s[slot, g, :, r:r + LANES] = eb[r:r + LANES, :].T

    def pv_stage(c, slot):
        q0 = c * CHUNK if isinstance(c, int) else pl.multiple_of(c * CHUNK, CHUNK)
        for g in groups:
            lanes = slice(g * gw, (g + 1) * gw)
            vb = v_s[pl.ds(q0, BAND_PAD), lanes]
            o4 = jnp.dot(pt_s[slot, g], vb, preferred_element_type=F32)
            out = o4[0:CHUNK]
            for hj in range(1, ATT_GROUP_HEADS):
                out = jnp.where(lane_head == hj, o4[hj * CHUNK:(hj + 1) * CHUNK], out)
            o_s[slot, :, lanes] = out

    def finalize_stage(c, slot):
        q0 = c * CHUNK if isinstance(c, int) else pl.multiple_of(c * CHUNK, CHUNK)
        o_ref[pl.ds(q0, CHUNK), :] = (_rms(o_s[slot]) * g_ref[...]).astype(o_ref.dtype)

    def block(c, slot, first=0, last=n_chunks - 1):
        if first <= c <= last:
            scores_stage(c, slot)
        if first <= c - 1 <= last:
            softmax_stage(1 - slot)
        if first <= c - 2 <= last:
            pv_stage(c - 2, slot)
        if first <= c - 3 <= last:
            finalize_stage(c - 3, 1 - slot)
        if first <= c <= last:
            mask_stage(c, slot)

    def steady_pair(j, carry):
        c = 3 + 2 * j
        for k in range(2):
            scores_stage(c + k, (1 + k) % 2)
            softmax_stage(k % 2)
            pv_stage(c + k - 2, (1 + k) % 2)
            finalize_stage(c + k - 3, k % 2)
            mask_stage(c + k, (1 + k) % 2)
        return carry

    n_pairs = (n_chunks - 4) // 2
    assert n_chunks >= 4 and n_chunks % 2 == 0
    for c in range(3):
        block(c, c % 2)
    lax.fori_loop(0, n_pairs, steady_pair, 0)
    for c in range(3 + 2 * n_pairs, n_chunks + 3):
        block(c, c % 2)


def _attention(q, k, v, bias_t, g):
    b, s, w = q.shape
    tq = ATT_TQ
    assert tq % ATT_LEFT == 0
    cur = lambda bi, i: (bi, i, 0)
    prev = lambda bi, i: (bi, jnp.maximum(i * (tq // ATT_LEFT) - 1, 0), 0)
    blk = (None, tq, w)
    left = (None, ATT_LEFT, w)
    return pl.pallas_call(
        _attn_kernel,
        grid=(b, s // tq),
        in_specs=[pl.BlockSpec(blk, cur),
                  pl.BlockSpec(left, prev), pl.BlockSpec(blk, cur),
                  pl.BlockSpec(left, prev), pl.BlockSpec(blk, cur),
                  _resident(bias_t.shape), _resident(g.shape)],
        out_specs=pl.BlockSpec(blk, cur),
        out_shape=jax.ShapeDtypeStruct((b, s, w), BF16),
        scratch_shapes=[pltpu.VMEM((ATT_LEFT + tq, w), BF16),
                        pltpu.VMEM((ATT_LEFT + tq + BAND_PAD - BAND, w), BF16),
                        pltpu.VMEM((2, ATT_GROUPS, BAND, ATT_GROUP_WIDTH), F32),
                        pltpu.VMEM((2, ATT_GROUPS, ATT_GROUP_WIDTH, BAND_PAD), BF16),
                        pltpu.VMEM((2, ATT_GROUPS, 8, ATT_GROUP_WIDTH), F32),
                        pltpu.VMEM((2, CHUNK, w), F32)],
        compiler_params=_params("arbitrary", "arbitrary"),
        name="chunk_attn",
    )(q, k, k, v, v, bias_t, g)


def _bias_table(rel_bias):
    n_rel = rel_bias.shape[1]
    n_f = BAND + REL_FUTURE
    f = jnp.concatenate([jnp.broadcast_to(rel_bias[:, n_rel - 1:], (ATT_HEADS, n_f - n_rel)),
                         rel_bias[:, ::-1]], axis=1).astype(F32) * LOG2E
    skew = jnp.broadcast_to(jnp.pad(f, ((0, 0), (0, 1)))[:, None, :], (ATT_HEADS, CHUNK, n_f + 1))
    skew = skew.reshape(ATT_HEADS, CHUNK * (n_f + 1))[:, :CHUNK * n_f].reshape(ATT_HEADS, CHUNK, n_f)
    bias = skew[:, :, REL_FUTURE:]
    bias = bias.reshape(ATT_GROUPS, ATT_GROUP_HEADS, CHUNK, BAND)
    return jnp.transpose(bias, (0, 3, 1, 2)).reshape(ATT_GROUPS, BAND, ATT_GROUP_WIDTH)


_HALO = 8


def _ssd_kernel(xbc_ref, z_ref, dt_ref, convw_ref, convb_ref, dtb_ref, a_ref, dskip_ref,
                expand_ref, g_ref, o_ref, raw_s, act_s, dt_s, y_s, h_s):
    i = pl.program_id(1)
    tb = SSD_TB
    L = SSD_L

    @pl.when(i == 0)
    def _():
        raw_s[0:_HALO, :] = jnp.zeros((_HALO, XBC_WIDTH), F32)
        h_s[...] = jnp.zeros_like(h_s)

    @pl.when(i > 0)
    def _():
        raw_s[0:_HALO, :] = raw_s[tb:tb + _HALO, :]

    raw_s[_HALO:, :] = xbc_ref[...].astype(F32)
    assert SSD_CONV == 4
    raw = raw_s[...]
    back1 = pltpu.roll(raw, 1, axis=0)
    near = convw_ref[3:4, :] * raw + convw_ref[2:3, :] * back1
    far = convw_ref[1:2, :] * raw + convw_ref[0:1, :] * back1
    conv = convb_ref[...] + near[_HALO:] + pltpu.roll(far, 2, axis=0)[_HALO:]
    act_s[...] = _silu(conv)

    dtr = dt_ref[...] + dtb_ref[...]
    dt_s[...] = jnp.maximum(dtr, 0.0) + jnp.log1p(jnp.exp(-jnp.abs(dtr)))

    a = -jnp.exp(a_ref[...]) * LOG2E
    ri = lax.broadcasted_iota(jnp.int32, (L, L), 0)
    ci = lax.broadcasted_iota(jnp.int32, (L, L), 1)
    causal = ri >= ci
    tril = causal.astype(BF16)
    low_lanes = lax.broadcasted_iota(jnp.int32, (L, LANES), 1) < SSD_HEAD_DIM
    expand = expand_ref[...]
    dskip = dskip_ref[...]
    bc0 = SSD_WIDTH
    cc0 = SSD_WIDTH + SSD_GROUPS * SSD_STATE

    def chunk_body(c, carry):
        r0 = pl.multiple_of(c * L, L)
        dtc = dt_s[pl.ds(r0, L), :]
        adt = dtc * a
        cs3 = jnp.dot(tril, jnp.concatenate(_split_bf16(adt, 3), axis=1), preferred_element_type=F32)
        cs = cs3[:, 0:LANES] + cs3[:, LANES:2 * LANES] + cs3[:, 2 * LANES:]
        cs_t = cs.T
        last = cs[L - 1:L, :]
        stacked = jnp.concatenate([dtc, jnp.exp2(last - cs), jnp.exp2(cs)], axis=0)
        wide = jnp.dot(jnp.concatenate(_split_bf16(stacked, 2), axis=1), expand,
                       preferred_element_type=F32)
        dt_w, decay_w, ecs_w = wide[0:L], wide[L:2 * L], wide[2 * L:3 * L]
        cdec = jnp.broadcast_to(jnp.exp2(last), (8, LANES))
        cdec_w = sum(jnp.dot(part, expand[:LANES], preferred_element_type=F32)
                     for part in _split_bf16(cdec, 3))[0:1]

        xs = act_s[pl.ds(r0, L), 0:SSD_WIDTH]
        xdt = xs * dt_w
        xdt_b = xdt.astype(BF16)
        xdec_b = (xdt * decay_w).astype(BF16)
        ys = []
        for g in range(SSD_GROUPS):
            gl = slice(g * GROUP_WIDTH, (g + 1) * GROUP_WIDTH)
            bm = act_s[pl.ds(r0, L), bc0 + g * SSD_STATE:bc0 + (g + 1) * SSD_STATE].astype(BF16)
            cm = act_s[pl.ds(r0, L), cc0 + g * SSD_STATE:cc0 + (g + 1) * SSD_STATE].astype(BF16)
            cb = lax.dot_general(cm, bm, (((1,), (1,)), ((), ())), preferred_element_type=F32)
            hprev = h_s[g]
            y_off = jnp.dot(cm, hprev.astype(BF16), preferred_element_type=F32) * ecs_w[:, gl]
            states = lax.dot_general(bm, xdec_b[:, gl], (((0,), (0,)), ((), ())),
                                     preferred_element_type=F32)
            h_s[g] = hprev * cdec_w[:, gl] + states
            yd = []
            for pr in range(GROUP_WIDTH // LANES):
                hp = g * (GROUP_WIDTH // LANES) + pr
                xp = xdt_b[:, hp * LANES:(hp + 1) * LANES]
                halves = []
                for hh in (2 * hp, 2 * hp + 1):
                    seg = jnp.exp2(jnp.where(causal, cs[:, hh:hh + 1] - cs_t[hh:hh + 1, :], -jnp.inf))
                    halves.append(jnp.dot((cb * seg).astype(BF16), xp, preferred_element_type=F32))
                yd.append(jnp.where(low_lanes, halves[0], halves[1]))
            ys.append(jnp.concatenate(yd, axis=-1) + y_off)
        y = jnp.concatenate(ys, axis=-1) + xs * dskip
        y_s[pl.ds(r0, L), :] = y
        return carry

    lax.fori_loop(0, tb // L, chunk_body, 0, unroll=True)
    gated = y_s[...] * _silu(z_ref[...].astype(F32))
    o_ref[...] = (_rms(gated) * g_ref[...]).astype(o_ref.dtype)


def _ssd(xbc, z, dt_raw, conv_w, conv_b, dt_bias, a_log, dskip_w, expand, g):
    b, s, _ = xbc.shape
    tb = SSD_TB
    row = lambda bi, i: (bi, i, 0)
    return pl.pallas_call(
        _ssd_kernel,
        grid=(b, s // tb),
        in_specs=[pl.BlockSpec((None, tb, XBC_WIDTH), row),
                  pl.BlockSpec((None, tb, SSD_WIDTH), row),
                  pl.BlockSpec((None, tb, LANES), row),
                  _resident(conv_w.shape), _resident(conv_b.shape), _resident(dt_bias.shape),
                  _resident(a_log.shape), _resident(dskip_w.shape), _resident(expand.shape),
                  _resident(g.shape)],
        out_specs=pl.BlockSpec((None, tb, SSD_WIDTH), row),
        out_shape=jax.ShapeDtypeStruct((b, s, SSD_WIDTH), BF16),
        scratch_shapes=[pltpu.VMEM((_HALO + tb, XBC_WIDTH), F32),
                        pltpu.VMEM((tb, XBC_WIDTH), F32),
                        pltpu.VMEM((tb, LANES), F32),
                        pltpu.VMEM((tb, SSD_WIDTH), F32),
                        pltpu.VMEM((SSD_GROUPS, SSD_STATE, GROUP_WIDTH), F32)],
        compiler_params=_params("arbitrary", "arbitrary"),
        name="ssd",
    )(xbc, z, dt_raw, conv_w, conv_b, dt_bias, a_log, dskip_w, expand, g)


def _outproj_kernel(att_ref, ssd_ref, x_ref, mods_ref, g_ref, w_ref, x1_ref, h_ref):
    for r0 in range(0, x_ref.shape[0], OUT_SUB):
        rows = slice(r0, r0 + OUT_SUB)
        mix = jnp.dot(att_ref[rows, :], w_ref[0:ATT_WIDTH, :], preferred_element_type=F32)
        mix = mix + jnp.dot(ssd_ref[rows, :], w_ref[ATT_WIDTH:, :], preferred_element_type=F32)
        x1 = x_ref[rows, :] + mods_ref[2:3, :] * mix
        x1_ref[rows, :] = x1
        h = _rms(x1) * g_ref[...]
        h_ref[rows, :] = (h * (1.0 + mods_ref[4:5, :]) + mods_ref[3:4, :]).astype(h_ref.dtype)


def _outproj(att, ssd, x, mods, g, w):
    b, s, d = x.shape
    tm = OUT_TM
    row = lambda bi, i: (bi, i, 0)
    return pl.pallas_call(
        _outproj_kernel,
        grid=(b, s // tm),
        in_specs=[pl.BlockSpec((None, tm, ATT_WIDTH), row),
                  pl.BlockSpec((None, tm, SSD_WIDTH), row),
                  pl.BlockSpec((None, tm, d), row),
                  pl.BlockSpec((None, N_MOD, d), lambda bi, i: (bi, 0, 0)),
                  _resident(g.shape), _resident(w.shape)],
        out_specs=[pl.BlockSpec((None, tm, d), row), pl.BlockSpec((None, tm, d), row)],
        out_shape=[jax.ShapeDtypeStruct((b, s, d), F32), jax.ShapeDtypeStruct((b, s, d), BF16)],
        compiler_params=_params("arbitrary", "arbitrary"),
        name="outproj",
    )(att, ssd, x, mods, g, w)


def _ffn_kernel(final_norm, h_ref, x1_ref, mods_ref, g_ref, wg_ref, wu_ref, wd_ref, o_ref):
    k = pl.program_id(2)

    def hidden_block(base_ref):
        h = h_ref[...]
        gate = jnp.dot(h, wg_ref[...], preferred_element_type=F32)
        up = jnp.dot(h, wu_ref[...], preferred_element_type=F32)
        act = (_silu(gate) * up).astype(BF16)
        for n0 in range(0, o_ref.shape[-1], FFN_TN):
            cols = slice(n0, n0 + FFN_TN)
            o_ref[:, cols] = base_ref[:, cols] + mods_ref[5:6, cols] * jnp.dot(
                act, wd_ref[:, cols], preferred_element_type=F32)

    pl.when(k == 0)(functools.partial(hidden_block, x1_ref))
    pl.when(k > 0)(functools.partial(hidden_block, o_ref))

    if final_norm:
        @pl.when(k == pl.num_programs(2) - 1)
        def _():
            o_ref[...] = _rms(o_ref[...]) * g_ref[...]


def _ffn(h, x1, mods, g_final, w_gate, w_up, w_down):
    b, s, d = x1.shape
    tm, th = FFN_TM, FFN_TH
    hid = w_gate.shape[1]
    row = lambda bi, i, k: (bi, i, 0)
    final_norm = g_final is not None
    g = g_final if final_norm else jnp.ones((1, d), F32)
    return pl.pallas_call(
        functools.partial(_ffn_kernel, final_norm),
        grid=(b, s // tm, hid // th),
        in_specs=[pl.BlockSpec((None, tm, d), row),
                  pl.BlockSpec((None, tm, d), row),
                  pl.BlockSpec((None, N_MOD, d), lambda bi, i, k: (bi, 0, 0)),
                  pl.BlockSpec(g.shape, lambda bi, i, k: (0, 0)),
                  pl.BlockSpec((d, th), lambda bi, i, k: (0, k)),
                  pl.BlockSpec((d, th), lambda bi, i, k: (0, k)),
                  pl.BlockSpec((th, d), lambda bi, i, k: (k, 0))],
        out_specs=pl.BlockSpec((None, tm, d), row),
        out_shape=jax.ShapeDtypeStruct((b, s, d), F32),
        compiler_params=_params("arbitrary", "arbitrary", "arbitrary"),
        name="ffn",
    )(h, x1, mods, g, w_gate, w_up, w_down)


def _pad_lanes(v):
    return jnp.pad(v.astype(F32), (0, LANES - v.shape[0])).reshape(1, LANES)


def kernel(x, c, w_ada, b_ada, g_mix, w_in, rel_bias, conv_w, conv_b, dt_bias, a_log, d_skip,
           g_att_out, g_ssd_out, w_out, g_ffn, w_gate, w_up, w_down, g_final):
    b, s, d = x.shape
    depth = w_ada.shape[0]
    n_main = sum(_PROJ_SPLITS)
    head_of_lane = np.arange(SSD_WIDTH) // SSD_HEAD_DIM
    expand = jnp.asarray(np.arange(2 * LANES)[:, None] % LANES == head_of_lane[None, :], BF16)
    c_pad = jnp.pad(c, ((0, 8 - b % 8 if b % 8 else 0), (0, 0)))
    for l in range(depth):
        mods = _adaln(c_pad, w_ada[l], b_ada[l].reshape(1, -1))[:b].reshape(b, N_MOD, d)
        n_in = w_in.shape[-1]
        q_scale = jnp.asarray(np.where(np.arange(n_in) < ATT_WIDTH, ATT_HEAD_DIM ** -0.5 * LOG2E, 1.0), F32)
        w_t = w_in[l].T
        w_proj = (w_t * q_scale[:, None]).astype(BF16)
        w_dt = jnp.pad(w_proj[n_main:], ((0, LANES - (n_in - n_main)), (0, 0)))
        q, k, v, z, xbc, dt_raw, w_o, w_g, w_u, w_d = _inproj(
            x, mods, g_mix[l].reshape(1, d), w_proj, w_dt, (w_out[l], w_gate[l], w_up[l], w_down[l]))
        att = _attention(q, k, v, _bias_table(rel_bias[l]), g_att_out[l].reshape(1, -1))
        ssd = _ssd(xbc, z, dt_raw, conv_w[l], conv_b[l].reshape(1, -1), _pad_lanes(dt_bias[l]),
                   _pad_lanes(a_log[l]), d_skip[l].astype(F32)[head_of_lane].reshape(1, -1), expand,
                   g_ssd_out[l].reshape(1, -1))
        x1, h2 = _outproj(att, ssd, x, mods, g_ffn[l].reshape(1, d), w_o)
        g_last = g_final.reshape(1, d) if l == depth - 1 else None
        x = _ffn(h2, x1, mods, g_last, w_g, w_u, w_d)
    return x
```

```python
import functools

import jax
import jax.numpy as jnp
import numpy as np
from jax import lax
from jax.experimental import pallas as pl
from jax.experimental.pallas import tpu as pltpu

F32 = jnp.float32
BF16 = jnp.bfloat16

D_MODEL = 2048
CHUNK = 64
LEFT_CHUNKS = 8
BAND = (LEFT_CHUNKS + 1) * CHUNK
ATT_HEADS = 16
ATT_HEAD_DIM = 64
ATT_WIDTH = ATT_HEADS * ATT_HEAD_DIM
REL_CLIP = 256
REL_FUTURE = CHUNK - 1
SSD_HEADS = 16
SSD_HEAD_DIM = 64
SSD_WIDTH = SSD_HEADS * SSD_HEAD_DIM
SSD_GROUPS = 2
SSD_STATE = 128
SSD_CONV = 4
XBC_WIDTH = SSD_WIDTH + 2 * SSD_GROUPS * SSD_STATE
GROUP_WIDTH = SSD_WIDTH // SSD_GROUPS
FFN_HIDDEN = 5632
N_MOD = 6
EPS = 1e-6

LANES = 128
BF16_SUBLANES = 16
MXU_WIDTH = 256
ATT_GROUP_WIDTH = MXU_WIDTH
ATT_GROUP_HEADS = ATT_GROUP_WIDTH // ATT_HEAD_DIM
ATT_GROUPS = ATT_WIDTH // ATT_GROUP_WIDTH
BAND_PAD = -(-BAND // LANES) * LANES
VMEM_LIMIT = 60 * 1024 * 1024

ADA_TN = 1024
PROJ_TM = 512
PROJ_TN = 512
ATT_TQ = 1024
ATT_LEFT = LEFT_CHUNKS * CHUNK
SSD_TB = 1024
SSD_L = 128
OUT_TM = 512
OUT_SUB = 256
FFN_TM = 1024
FFN_TH = 512
FFN_TN = 512


def _params(*sem):
    return pltpu.CompilerParams(dimension_semantics=sem, vmem_limit_bytes=VMEM_LIMIT)


def _resident(shape):
    nd = len(shape)
    return pl.BlockSpec(shape, lambda *_: (0,) * nd, pipeline_mode=pl.Buffered(1))


def _rms(x):
    return x * lax.rsqrt(jnp.mean(x * x, axis=-1, keepdims=True) + EPS)


def _silu(x):
    h = 0.5 * x
    return h + h * jnp.tanh(h)


def _split_bf16(x, terms):
    parts = []
    for _ in range(terms):
        p = x.astype(BF16)
        parts.append(p)
        x = x - p.astype(F32)
    return parts


def _adaln_kernel(c_ref, w_ref, b_ref, o_ref):
    cond = _silu(c_ref[...])
    o_ref[...] = jnp.dot(cond.astype(BF16), w_ref[...].astype(BF16),
                         preferred_element_type=F32) + b_ref[...]


def _adaln(c_pad, w, b):
    rows, d = c_pad.shape
    n = w.shape[1]
    return pl.pallas_call(
        _adaln_kernel,
        grid=(n // ADA_TN,),
        in_specs=[pl.BlockSpec((rows, d), lambda j: (0, 0)),
                  pl.BlockSpec((d, ADA_TN), lambda j: (0, j)),
                  pl.BlockSpec((1, ADA_TN), lambda j: (0, j))],
        out_specs=pl.BlockSpec((rows, ADA_TN), lambda j: (0, j)),
        out_shape=jax.ShapeDtypeStruct((rows, n), F32),
        compiler_params=_params("arbitrary"),
        name="adaln",
    )(c_pad, w, b)


_PROJ_SPLITS = (ATT_WIDTH, ATT_WIDTH, ATT_WIDTH, SSD_WIDTH, XBC_WIDTH)
_NT = (((1,), (1,)), ((), ()))


def _cast_rows(n_rows, n_steps):
    rows = next(r for r in range(BF16_SUBLANES, n_rows + 1, BF16_SUBLANES)
                if n_rows % r == 0 and r * n_steps >= n_rows)
    return rows


def _inproj_kernel(n_cast, x_ref, mods_ref, g_ref, w_ref, wdt_ref, *refs):
    cast_in, refs = refs[:n_cast], refs[n_cast:]
    (q_ref, k_ref, v_ref, z_ref, xbc_ref, dt_ref), refs = refs[:6], refs[6:]
    cast_out, (h_s,) = refs[:n_cast], refs[n_cast:]
    for src, dst in zip(cast_in, cast_out):
        dst[...] = src[...].astype(dst.dtype)
    x = x_ref[...]
    h = _rms(x) * g_ref[...]
    h = h * (1.0 + mods_ref[1:2, :]) + mods_ref[0:1, :]
    h_s[...] = h.astype(BF16)
    col = 0
    for ref, width in zip((q_ref, k_ref, v_ref, z_ref, xbc_ref), _PROJ_SPLITS):
        for j in range(0, width, PROJ_TN):
            ref[:, j:j + PROJ_TN] = lax.dot_general(
                h_s[...], w_ref[col + j:col + j + PROJ_TN, :], _NT,
                preferred_element_type=F32).astype(ref.dtype)
        col += width
    dt_ref[...] = lax.dot_general(h_s[...], wdt_ref[...], _NT, preferred_element_type=F32)


def _inproj(x, mods, g, w, w_dt, to_cast):
    b, s, d = x.shape
    tm = PROJ_TM
    steps_per_seq = s // tm
    row = lambda bi, i: (bi, i, 0)
    out_shape = [jax.ShapeDtypeStruct((b, s, w), BF16) for w in _PROJ_SPLITS]
    out_shape.append(jax.ShapeDtypeStruct((b, s, LANES), F32))
    out_specs = [pl.BlockSpec((None, tm, w), row) for w in _PROJ_SPLITS]
    out_specs.append(pl.BlockSpec((None, tm, LANES), row))
    cast_specs = []
    for a in to_cast:
        rows = _cast_rows(a.shape[0], b * steps_per_seq)
        blk = functools.partial(lambda n_blk, bi, i: (jnp.minimum(bi * steps_per_seq + i, n_blk - 1), 0),
                                a.shape[0] // rows)
        cast_specs.append(pl.BlockSpec((rows, a.shape[1]), blk))
        out_shape.append(jax.ShapeDtypeStruct(a.shape, BF16))
    return pl.pallas_call(
        functools.partial(_inproj_kernel, len(to_cast)),
        grid=(b, steps_per_seq),
        in_specs=[pl.BlockSpec((None, tm, d), row),
                  pl.BlockSpec((None, N_MOD, d), lambda bi, i: (bi, 0, 0)),
                  _resident(g.shape), _resident(w.shape), _resident(w_dt.shape)] + cast_specs,
        out_specs=out_specs + cast_specs,
        out_shape=out_shape,
        scratch_shapes=[pltpu.VMEM((tm, d), BF16)],
        compiler_params=_params("arbitrary", "arbitrary"),
        name="inproj",
    )(x, mods, g, w, w_dt, *to_cast)


LOG2E = 1.4426950408889634


def _attn_kernel(q_ref, kp_ref, kc_ref, vp_ref, vc_ref, bias_ref, g_ref, o_ref,
                 k_s, v_s, s_s, pt_s, max_s, den_s, o_s, inv_s):
    i = pl.program_id(1)
    tq = ATT_TQ
    k_s[0:ATT_LEFT, :] = kp_ref[...]
    k_s[ATT_LEFT:, :] = kc_ref[...]
    v_s[0:ATT_LEFT, :] = vp_ref[...]
    v_s[ATT_LEFT:ATT_LEFT + tq, :] = vc_ref[...]
    v_s[ATT_LEFT + tq:, :] = jnp.zeros((BAND_PAD - BAND, ATT_WIDTH), BF16)

    gw = ATT_GROUP_WIDTH
    rr = lax.broadcasted_iota(jnp.int32, (gw, gw), 0) // CHUNK
    ll = lax.broadcasted_iota(jnp.int32, (gw, gw), 1) // ATT_HEAD_DIM
    own_head = rr == ll
    lane_head = lax.broadcasted_iota(jnp.int32, (CHUNK, gw), 1) // ATT_HEAD_DIM
    groups = range(ATT_GROUPS)
    n_chunks = tq // CHUNK

    def scores_stage(c, slot):
        q0 = c * CHUNK if isinstance(c, int) else pl.multiple_of(c * CHUNK, CHUNK)
        for g in groups:
            lanes = slice(g * gw, (g + 1) * gw)
            qg = q_ref[pl.ds(q0, CHUNK), lanes]
            q_bd = jnp.where(own_head, jnp.concatenate([qg] * ATT_GROUP_HEADS, axis=0), jnp.zeros((), BF16))
            kb = k_s[pl.ds(q0, BAND), lanes]
            st = lax.dot_general(kb, q_bd, (((1,), (1,)), ((), ())),
                                 preferred_element_type=F32) + bias_ref[g]
            s_s[slot, g] = st
            max_s[slot, g] = jnp.broadcast_to(jnp.max(st, axis=0, keepdims=True), (8, gw))

    def mask_stage(c, slot):
        n_before = jnp.maximum(LEFT_CHUNKS - (i * n_chunks + c), 0)

        def mask_slab(j, carry):
            r0 = pl.multiple_of(j * CHUNK, CHUNK)
            for g in groups:
                s_s[slot, g, pl.ds(r0, CHUNK), :] = jnp.full((CHUNK, gw), -jnp.inf, F32)
            return carry

        lax.fori_loop(0, n_before, mask_slab, 0)

        @pl.when(n_before > 0)
        def _():
            for g in groups:
                max_s[slot, g] = jnp.broadcast_to(jnp.max(s_s[slot, g], axis=0, keepdims=True), (8, gw))

    def softmax_stage(slot):
        for g in groups:
            e = jnp.exp2(s_s[slot, g] - max_s[slot, g, 0:1, :])
            den_s[slot, g] = jnp.broadcast_to(jnp.sum(e, axis=0, keepdims=True), (8, gw))
            eb = jnp.concatenate([e.astype(BF16), jnp.zeros((BAND_PAD - BAND, gw), BF16)], axis=0)
            for r in range(0, BAND_PAD, LANES):
                pt_s[slot, g, :, r:r + LANES] = eb[r:r + LANES, :].T

    def pv_stage(c, slot):
        q0 = c * CHUNK if isinstance(c, int) else pl.multiple_of(c * CHUNK, CHUNK)
        for g in groups:
            lanes = slice(g * gw, (g + 1) * gw)
            vb = v_s[pl.ds(q0, BAND_PAD), lanes]
            o4 = jnp.dot(pt_s[slot, g], vb, preferred_element_type=F32)
            out = o4[0:CHUNK]
            for hj in range(1, ATT_GROUP_HEADS):
                out = jnp.where(lane_head == hj, o4[hj * CHUNK:(hj + 1) * CHUNK], out)
            o_s[slot, :, lanes] = out
            inv_s[slot, g] = 1.0 / den_s[slot, g]

    def finalize_stage(c, slot):
        q0 = c * CHUNK if isinstance(c, int) else pl.multiple_of(c * CHUNK, CHUNK)
        scales = []
        for g in groups:
            inv_den = inv_s[slot, g, 0:1, :]
            scale = None
            for half in range(gw // LANES):
                col = jnp.broadcast_to(inv_den[:, half * LANES:(half + 1) * LANES], (LANES, LANES)).T
                for j in range(LANES // CHUNK):
                    hj = half * (LANES // CHUNK) + j
                    blk = jnp.concatenate([col[j * CHUNK:(j + 1) * CHUNK]] * (gw // LANES), axis=1)
                    scale = blk if scale is None else jnp.where(lane_head == hj, blk, scale)
            scales.append(scale)
        att = o_s[slot] * jnp.concatenate(scales, axis=-1)
        o_ref[pl.ds(q0, CHUNK), :] = (_rms(att) * g_ref[...]).astype(o_ref.dtype)

    def block(c, slot, first=0, last=n_chunks - 1):
        if first <= c <= last:
            scores_stage(c, slot)
        if first <= c - 1 <= last:
            softmax_stage(1 - slot)
        if first <= c - 2 <= last:
            pv_stage(c - 2, slot)
        if first <= c - 3 <= last:
            finalize_stage(c - 3, 1 - slot)
        if first <= c <= last:
            mask_stage(c, slot)

    def steady_pair(j, carry):
        c = 3 + 2 * j
        for k in range(2):
            scores_stage(c + k, (1 + k) % 2)
            softmax_stage(k % 2)
            pv_stage(c + k - 2, (1 + k) % 2)
            finalize_stage(c + k - 3, k % 2)
            mask_stage(c + k, (1 + k) % 2)
        return carry

    n_pairs = (n_chunks - 4) // 2
    assert n_chunks >= 4 and n_chunks % 2 == 0
    for c in range(3):
        block(c, c % 2)
    lax.fori_loop(0, n_pairs, steady_pair, 0)
    for c in range(3 + 2 * n_pairs, n_chunks + 3):
        block(c, c % 2)


def _attention(q, k, v, bias_t, g):
    b, s, w = q.shape
    tq = ATT_TQ
    assert tq % ATT_LEFT == 0
    cur = lambda bi, i: (bi, i, 0)
    prev = lambda bi, i: (bi, jnp.maximum(i * (tq // ATT_LEFT) - 1, 0), 0)
    blk = (None, tq, w)
    left = (None, ATT_LEFT, w)
    return pl.pallas_call(
        _attn_kernel,
        grid=(b, s // tq),
        in_specs=[pl.BlockSpec(blk, cur),
                  pl.BlockSpec(left, prev), pl.BlockSpec(blk, cur),
                  pl.BlockSpec(left, prev), pl.BlockSpec(blk, cur),
                  _resident(bias_t.shape), _resident(g.shape)],
        out_specs=pl.BlockSpec(blk, cur),
        out_shape=jax.ShapeDtypeStruct((b, s, w), BF16),
        scratch_shapes=[pltpu.VMEM((ATT_LEFT + tq, w), BF16),
                        pltpu.VMEM((ATT_LEFT + tq + BAND_PAD - BAND, w), BF16),
                        pltpu.VMEM((2, ATT_GROUPS, BAND, ATT_GROUP_WIDTH), F32),
                        pltpu.VMEM((2, ATT_GROUPS, ATT_GROUP_WIDTH, BAND_PAD), BF16),
                        pltpu.VMEM((2, ATT_GROUPS, 8, ATT_GROUP_WIDTH), F32),
                        pltpu.VMEM((2, ATT_GROUPS, 8, ATT_GROUP_WIDTH), F32),
                        pltpu.VMEM((2, CHUNK, w), F32),
                        pltpu.VMEM((2, ATT_GROUPS, 8, ATT_GROUP_WIDTH), F32)],
        compiler_params=_params("arbitrary", "arbitrary"),
        name="chunk_attn",
    )(q, k, k, v, v, bias_t, g)


def _bias_table(rel_bias):
    n_rel = rel_bias.shape[1]
    n_f = BAND + REL_FUTURE
    f = jnp.concatenate([jnp.broadcast_to(rel_bias[:, n_rel - 1:], (ATT_HEADS, n_f - n_rel)),
                         rel_bias[:, ::-1]], axis=1).astype(F32) * LOG2E
    skew = jnp.broadcast_to(jnp.pad(f, ((0, 0), (0, 1)))[:, None, :], (ATT_HEADS, CHUNK, n_f + 1))
    skew = skew.reshape(ATT_HEADS, CHUNK * (n_f + 1))[:, :CHUNK * n_f].reshape(ATT_HEADS, CHUNK, n_f)
    bias = skew[:, :, REL_FUTURE:]
    bias = bias.reshape(ATT_GROUPS, ATT_GROUP_HEADS, CHUNK, BAND)
    return jnp.transpose(bias, (0, 3, 1, 2)).reshape(ATT_GROUPS, BAND, ATT_GROUP_WIDTH)


_HALO = 8


def _ssd_kernel(xbc_ref, z_ref, dt_ref, convw_ref, convb_ref, dtb_ref, a_ref, dskip_ref,
                expand_ref, g_ref, o_ref, raw_s, act_s, dt_s, y_s, h_s):
    i = pl.program_id(1)
    tb = SSD_TB
    L = SSD_L

    @pl.when(i == 0)
    def _():
        raw_s[0:_HALO, :] = jnp.zeros((_HALO, XBC_WIDTH), F32)
        h_s[...] = jnp.zeros_like(h_s)

    @pl.when(i > 0)
    def _():
        raw_s[0:_HALO, :] = raw_s[tb:tb + _HALO, :]

    raw_s[_HALO:, :] = xbc_ref[...].astype(F32)
    assert SSD_CONV == 4
    raw = raw_s[...]
    back1 = pltpu.roll(raw, 1, axis=0)
    near = convw_ref[3:4, :] * raw + convw_ref[2:3, :] * back1
    far = convw_ref[1:2, :] * raw + convw_ref[0:1, :] * back1
    conv = convb_ref[...] + near[_HALO:] + pltpu.roll(far, 2, axis=0)[_HALO:]
    act_s[...] = _silu(conv)

    dtr = dt_ref[...] + dtb_ref[...]
    dt_s[...] = jnp.maximum(dtr, 0.0) + jnp.log1p(jnp.exp(-jnp.abs(dtr)))

    a = -jnp.exp(a_ref[...]) * LOG2E
    ri = lax.broadcasted_iota(jnp.int32, (L, L), 0)
    ci = lax.broadcasted_iota(jnp.int32, (L, L), 1)
    causal = ri >= ci
    tril = causal.astype(BF16)
    low_lanes = lax.broadcasted_iota(jnp.int32, (L, LANES), 1) < SSD_HEAD_DIM
    expand = expand_ref[...]
    dskip = dskip_ref[...]
    bc0 = SSD_WIDTH
    cc0 = SSD_WIDTH + SSD_GROUPS * SSD_STATE

    def chunk_body(c, carry):
        r0 = pl.multiple_of(c * L, L)
        dtc = dt_s[pl.ds(r0, L), :]
        adt = dtc * a
        cs3 = jnp.dot(tril, jnp.concatenate(_split_bf16(adt, 3), axis=1), preferred_element_type=F32)
        cs = cs3[:, 0:LANES] + cs3[:, LANES:2 * LANES] + cs3[:, 2 * LANES:]
        cs_t = cs.T
        last = cs[L - 1:L, :]
        stacked = jnp.concatenate([dtc, jnp.exp2(last - cs), jnp.exp2(cs)], axis=0)
        wide = jnp.dot(jnp.concatenate(_split_bf16(stacked, 2), axis=1), expand,
                       preferred_element_type=F32)
        dt_w, decay_w, ecs_w = wide[0:L], wide[L:2 * L], wide[2 * L:3 * L]
        cdec = jnp.broadcast_to(jnp.exp2(last), (8, LANES))
        cdec_w = sum(jnp.dot(part, expand[:LANES], preferred_element_type=F32)
                     for part in _split_bf16(cdec, 3))[0:1]

        xs = act_s[pl.ds(r0, L), 0:SSD_WIDTH]
        xdt = xs * dt_w
        xdt_b = xdt.astype(BF16)
        xdec_b = (xdt * decay_w).astype(BF16)
        ys = []
        for g in range(SSD_GROUPS):
            gl = slice(g * GROUP_WIDTH, (g + 1) * GROUP_WIDTH)
            bm = act_s[pl.ds(r0, L), bc0 + g * SSD_STATE:bc0 + (g + 1) * SSD_STATE].astype(BF16)
            cm = act_s[pl.ds(r0, L), cc0 + g * SSD_STATE:cc0 + (g + 1) * SSD_STATE].astype(BF16)
            cb = lax.dot_general(cm, bm, (((1,), (1,)), ((), ())), preferred_element_type=F32)
            hprev = h_s[g]
            y_off = jnp.dot(cm, hprev.astype(BF16), preferred_element_type=F32) * ecs_w[:, gl]
            states = lax.dot_general(bm, xdec_b[:, gl], (((0,), (0,)), ((), ())),
                                     preferred_element_type=F32)
            h_s[g] = hprev * cdec_w[:, gl] + states
            yd = []
            for pr in range(GROUP_WIDTH // LANES):
                hp = g * (GROUP_WIDTH // LANES) + pr
                xp = xdt_b[:, hp * LANES:(hp + 1) * LANES]
                halves = []
                for hh in (2 * hp, 2 * hp + 1):
                    seg = jnp.exp2(jnp.where(causal, cs[:, hh:hh + 1] - cs_t[hh:hh + 1, :], -jnp.inf))
                    halves.append(jnp.dot((cb * seg).astype(BF16), xp, preferred_element_type=F32))
                yd.append(jnp.where(low_lanes, halves[0], halves[1]))
            ys.append(jnp.concatenate(yd, axis=-1) + y_off)
        y = jnp.concatenate(ys, axis=-1) + xs * dskip
        y_s[pl.ds(r0, L), :] = y
        return carry

    lax.fori_loop(0, tb // L, chunk_body, 0, unroll=True)
    gated = y_s[...] * _silu(z_ref[...].astype(F32))
    o_ref[...] = (_rms(gated) * g_ref[...]).astype(o_ref.dtype)


def _ssd(xbc, z, dt_raw, conv_w, conv_b, dt_bias, a_log, dskip_w, expand, g):
    b, s, _ = xbc.shape
    tb = SSD_TB
    row = lambda bi, i: (bi, i, 0)
    return pl.pallas_call(
        _ssd_kernel,
        grid=(b, s // tb),
        in_specs=[pl.BlockSpec((None, tb, XBC_WIDTH), row),
                  pl.BlockSpec((None, tb, SSD_WIDTH), row),
                  pl.BlockSpec((None, tb, LANES), row),
                  _resident(conv_w.shape), _resident(conv_b.shape), _resident(dt_bias.shape),
                  _resident(a_log.shape), _resident(dskip_w.shape), _resident(expand.shape),
                  _resident(g.shape)],
        out_specs=pl.BlockSpec((None, tb, SSD_WIDTH), row),
        out_shape=jax.ShapeDtypeStruct((b, s, SSD_WIDTH), BF16),
        scratch_shapes=[pltpu.VMEM((_HALO + tb, XBC_WIDTH), F32),
                        pltpu.VMEM((tb, XBC_WIDTH), F32),
                        pltpu.VMEM((tb, LANES), F32),
                        pltpu.VMEM((tb, SSD_WIDTH), F32),
                        pltpu.VMEM((SSD_GROUPS, SSD_STATE, GROUP_WIDTH), F32)],
        compiler_params=_params("arbitrary", "arbitrary"),
        name="ssd",
    )(xbc, z, dt_raw, conv_w, conv_b, dt_bias, a_log, dskip_w, expand, g)


def _outproj_kernel(att_ref, ssd_ref, x_ref, mods_ref, g_ref, w_ref, x1_ref, h_ref):
    for r0 in range(0, x_ref.shape[0], OUT_SUB):
        rows = slice(r0, r0 + OUT_SUB)
        mix = jnp.dot(att_ref[rows, :], w_ref[0:ATT_WIDTH, :], preferred_element_type=F32)
        mix = mix + jnp.dot(ssd_ref[rows, :], w_ref[ATT_WIDTH:, :], preferred_element_type=F32)
        x1 = x_ref[rows, :] + mods_ref[2:3, :] * mix
        x1_ref[rows, :] = x1
        h = _rms(x1) * g_ref[...]
        h_ref[rows, :] = (h * (1.0 + mods_ref[4:5, :]) + mods_ref[3:4, :]).astype(h_ref.dtype)


def _outproj(att, ssd, x, mods, g, w):
    b, s, d = x.shape
    tm = OUT_TM
    row = lambda bi, i: (bi, i, 0)
    return pl.pallas_call(
        _outproj_kernel,
        grid=(b, s // tm),
        in_specs=[pl.BlockSpec((None, tm, ATT_WIDTH), row),
                  pl.BlockSpec((None, tm, SSD_WIDTH), row),
                  pl.BlockSpec((None, tm, d), row),
                  pl.BlockSpec((None, N_MOD, d), lambda bi, i: (bi, 0, 0)),
                  _resident(g.shape), _resident(w.shape)],
        out_specs=[pl.BlockSpec((None, tm, d), row), pl.BlockSpec((None, tm, d), row)],
        out_shape=[jax.ShapeDtypeStruct((b, s, d), F32), jax.ShapeDtypeStruct((b, s, d), BF16)],
        compiler_params=_params("arbitrary", "arbitrary"),
        name="outproj",
    )(att, ssd, x, mods, g, w)


def _ffn_kernel(final_norm, h_ref, x1_ref, mods_ref, g_ref, wg_ref, wu_ref, wd_ref, o_ref):
    k = pl.program_id(2)

    def hidden_block(base_ref):
        h = h_ref[...]
        gate = jnp.dot(h, wg_ref[...], preferred_element_type=F32)
        up = jnp.dot(h, wu_ref[...], preferred_element_type=F32)
        act = (_silu(gate) * up).astype(BF16)
        for n0 in range(0, o_ref.shape[-1], FFN_TN):
            cols = slice(n0, n0 + FFN_TN)
            o_ref[:, cols] = base_ref[:, cols] + mods_ref[5:6, cols] * jnp.dot(
                act, wd_ref[:, cols], preferred_element_type=F32)

    pl.when(k == 0)(functools.partial(hidden_block, x1_ref))
    pl.when(k > 0)(functools.partial(hidden_block, o_ref))

    if final_norm:
        @pl.when(k == pl.num_programs(2) - 1)
        def _():
            o_ref[...] = _rms(o_ref[...]) * g_ref[...]


def _ffn(h, x1, mods, g_final, w_gate, w_up, w_down):
    b, s, d = x1.shape
    tm, th = FFN_TM, FFN_TH
    hid = w_gate.shape[1]
    row = lambda bi, i, k: (bi, i, 0)
    final_norm = g_final is not None
    g = g_final if final_norm else jnp.ones((1, d), F32)
    return pl.pallas_call(
        functools.partial(_ffn_kernel, final_norm),
        grid=(b, s // tm, hid // th),
        in_specs=[pl.BlockSpec((None, tm, d), row),
                  pl.BlockSpec((None, tm, d), row),
                  pl.BlockSpec((None, N_MOD, d), lambda bi, i, k: (bi, 0, 0)),
                  pl.BlockSpec(g.shape, lambda bi, i, k: (0, 0)),
                  pl.BlockSpec((d, th), lambda bi, i, k: (0, k)),
                  pl.BlockSpec((d, th), lambda bi, i, k: (0, k)),
                  pl.BlockSpec((th, d), lambda bi, i, k: (k, 0))],
        out_specs=pl.BlockSpec((None, tm, d), row),
        out_shape=jax.ShapeDtypeStruct((b, s, d), F32),
        compiler_params=_params("arbitrary", "arbitrary", "arbitrary"),
        name="ffn",
    )(h, x1, mods, g, w_gate, w_up, w_down)


def _pad_lanes(v):
    return jnp.pad(v.astype(F32), (0, LANES - v.shape[0])).reshape(1, LANES)


def kernel(x, c, w_ada, b_ada, g_mix, w_in, rel_bias, conv_w, conv_b, dt_bias, a_log, d_skip,
           g_att_out, g_ssd_out, w_out, g_ffn, w_gate, w_up, w_down, g_final):
    b, s, d = x.shape
    depth = w_ada.shape[0]
    n_main = sum(_PROJ_SPLITS)
    head_of_lane = np.arange(SSD_WIDTH) // SSD_HEAD_DIM
    expand = jnp.asarray(np.arange(2 * LANES)[:, None] % LANES == head_of_lane[None, :], BF16)
    c_pad = jnp.pad(c, ((0, 8 - b % 8 if b % 8 else 0), (0, 0)))
    for l in range(depth):
        mods = _adaln(c_pad, w_ada[l], b_ada[l].reshape(1, -1))[:b].reshape(b, N_MOD, d)
        n_in = w_in.shape[-1]
        q_scale = jnp.asarray(np.where(np.arange(n_in) < ATT_WIDTH, ATT_HEAD_DIM ** -0.5 * LOG2E, 1.0), F32)
        w_t = w_in[l].T
        w_proj = (w_t * q_scale[:, None]).astype(BF16)
        w_dt = jnp.pad(w_proj[n_main:], ((0, LANES - (n_in - n_main)), (0, 0)))
        q, k, v, z, xbc, dt_raw, w_o, w_g, w_u, w_d = _inproj(
            x, mods, g_mix[l].reshape(1, d), w_proj, w_dt, (w_out[l], w_gate[l], w_up[l], w_down[l]))
        att = _attention(q, k, v, _bias_table(rel_bias[l]), g_att_out[l].reshape(1, -1))
        ssd = _ssd(xbc, z, dt_raw, conv_w[l], conv_b[l].reshape(1, -1), _pad_lanes(dt_bias[l]),
                   _pad_lanes(a_log[l]), d_skip[l].astype(F32)[head_of_lane].reshape(1, -1), expand,
                   g_ssd_out[l].reshape(1, -1))
        x1, h2 = _outproj(att, ssd, x, mods, g_ffn[l].reshape(1, d), w_o)
        g_last = g_final.reshape(1, d) if l == depth - 1 else None
        x = _ffn(h2, x1, mods, g_last, w_g, w_u, w_d)
    return x
```

```python
import functools

import jax
import jax.numpy as jnp
import numpy as np
from jax import lax
from jax.experimental import pallas as pl
from jax.experimental.pallas import tpu as pltpu

F32 = jnp.float32
BF16 = jnp.bfloat16

D_MODEL = 2048
CHUNK = 64
LEFT_CHUNKS = 8
BAND = (LEFT_CHUNKS + 1) * CHUNK
ATT_HEADS = 16
ATT_HEAD_DIM = 64
ATT_WIDTH = ATT_HEADS * ATT_HEAD_DIM
REL_CLIP = 256
REL_FUTURE = CHUNK - 1
SSD_HEADS = 16
SSD_HEAD_DIM = 64
SSD_WIDTH = SSD_HEADS * SSD_HEAD_DIM
SSD_GROUPS = 2
SSD_STATE = 128
SSD_CONV = 4
XBC_WIDTH = SSD_WIDTH + 2 * SSD_GROUPS * SSD_STATE
GROUP_WIDTH = SSD_WIDTH // SSD_GROUPS
FFN_HIDDEN = 5632
N_MOD = 6
EPS = 1e-6

LANES = 128
BF16_SUBLANES = 16
MXU_WIDTH = 256
ATT_GROUP_WIDTH = MXU_WIDTH
ATT_GROUP_HEADS = ATT_GROUP_WIDTH // ATT_HEAD_DIM
ATT_GROUPS = ATT_WIDTH // ATT_GROUP_WIDTH
BAND_PAD = -(-BAND // LANES) * LANES
VMEM_LIMIT = 60 * 1024 * 1024

ADA_TN = 1024
PROJ_TM = 512
PROJ_TN = 512
ATT_TQ = 1024
ATT_LEFT = LEFT_CHUNKS * CHUNK
SSD_TB = 1024
SSD_L = 128
OUT_TM = 512
OUT_SUB = 256
FFN_TM = 1024
FFN_TH = 512
FFN_TN = 512


def _params(*sem):
    return pltpu.CompilerParams(dimension_semantics=sem, vmem_limit_bytes=VMEM_LIMIT)


def _resident(shape):
    nd = len(shape)
    return pl.BlockSpec(shape, lambda *_: (0,) * nd, pipeline_mode=pl.Buffered(1))


def _rms(x):
    return x * lax.rsqrt(jnp.mean(x * x, axis=-1, keepdims=True) + EPS)


def _silu(x):
    h = 0.5 * x
    return h + h * jnp.tanh(h)


def _split_bf16(x, terms):
    parts = []
    for _ in range(terms):
        p = x.astype(BF16)
        parts.append(p)
        x = x - p.astype(F32)
    return parts


def _adaln_kernel(c_ref, w_ref, b_ref, o_ref):
    cond = _silu(c_ref[...])
    o_ref[...] = jnp.dot(cond.astype(BF16), w_ref[...].astype(BF16),
                         preferred_element_type=F32) + b_ref[...]


def _adaln(c_pad, w, b):
    rows, d = c_pad.shape
    n = w.shape[1]
    return pl.pallas_call(
        _adaln_kernel,
        grid=(n // ADA_TN,),
        in_specs=[pl.BlockSpec((rows, d), lambda j: (0, 0)),
                  pl.BlockSpec((d, ADA_TN), lambda j: (0, j)),
                  pl.BlockSpec((1, ADA_TN), lambda j: (0, j))],
        out_specs=pl.BlockSpec((rows, ADA_TN), lambda j: (0, j)),
        out_shape=jax.ShapeDtypeStruct((rows, n), F32),
        compiler_params=_params("arbitrary"),
        name="adaln",
    )(c_pad, w, b)


_PROJ_SPLITS = (ATT_WIDTH, ATT_WIDTH, ATT_WIDTH, SSD_WIDTH, XBC_WIDTH)
_NT = (((1,), (1,)), ((), ()))


def _cast_rows(n_rows, n_steps):
    rows = next(r for r in range(BF16_SUBLANES, n_rows + 1, BF16_SUBLANES)
                if n_rows % r == 0 and r * n_steps >= n_rows)
    return rows


def _inproj_kernel(n_cast, x_ref, mods_ref, g_ref, w_ref, wdt_ref, *refs):
    cast_in, refs = refs[:n_cast], refs[n_cast:]
    (q_ref, k_ref, v_ref, z_ref, xbc_ref, dt_ref), refs = refs[:6], refs[6:]
    cast_out, (h_s,) = refs[:n_cast], refs[n_cast:]
    for src, dst in zip(cast_in, cast_out):
        dst[...] = src[...].astype(dst.dtype)
    x = x_ref[...]
    h = _rms(x) * g_ref[...]
    h = h * (1.0 + mods_ref[1:2, :]) + mods_ref[0:1, :]
    h_s[...] = h.astype(BF16)
    col = 0
    for ref, width in zip((q_ref, k_ref, v_ref, z_ref, xbc_ref), _PROJ_SPLITS):
        for j in range(0, width, PROJ_TN):
            ref[:, j:j + PROJ_TN] = lax.dot_general(
                h_s[...], w_ref[col + j:col + j + PROJ_TN, :], _NT,
                preferred_element_type=F32).astype(ref.dtype)
        col += width
    dt_ref[...] = lax.dot_general(h_s[...], wdt_ref[...], _NT, preferred_element_type=F32)


def _inproj(x, mods, g, w, w_dt, to_cast):
    b, s, d = x.shape
    tm = PROJ_TM
    steps_per_seq = s // tm
    row = lambda bi, i: (bi, i, 0)
    out_shape = [jax.ShapeDtypeStruct((b, s, w), BF16) for w in _PROJ_SPLITS]
    out_shape.append(jax.ShapeDtypeStruct((b, s, LANES), F32))
    out_specs = [pl.BlockSpec((None, tm, w), row) for w in _PROJ_SPLITS]
    out_specs.append(pl.BlockSpec((None, tm, LANES), row))
    cast_specs = []
    for a in to_cast:
        rows = _cast_rows(a.shape[0], b * steps_per_seq)
        blk = functools.partial(lambda n_blk, bi, i: (jnp.minimum(bi * steps_per_seq + i, n_blk - 1), 0),
                                a.shape[0] // rows)
        cast_specs.append(pl.BlockSpec((rows, a.shape[1]), blk))
        out_shape.append(jax.ShapeDtypeStruct(a.shape, BF16))
    return pl.pallas_call(
        functools.partial(_inproj_kernel, len(to_cast)),
        grid=(b, steps_per_seq),
        in_specs=[pl.BlockSpec((None, tm, d), row),
                  pl.BlockSpec((None, N_MOD, d), lambda bi, i: (bi, 0, 0)),
                  _resident(g.shape), _resident(w.shape), _resident(w_dt.shape)] + cast_specs,
        out_specs=out_specs + cast_specs,
        out_shape=out_shape,
        scratch_shapes=[pltpu.VMEM((tm, d), BF16)],
        compiler_params=_params("arbitrary", "arbitrary"),
        name="inproj",
    )(x, mods, g, w, w_dt, *to_cast)


LOG2E = 1.4426950408889634


def _attn_kernel(q_ref, kp_ref, kc_ref, vp_ref, vc_ref, bias_ref, g_ref, o_ref,
                 k_s, v_s, s_s, pt_s, max_s, den_s, o_s, inv_s):
    i = pl.program_id(1)
    tq = ATT_TQ
    k_s[0:ATT_LEFT, :] = kp_ref[...]
    k_s[ATT_LEFT:, :] = kc_ref[...]
    v_s[0:ATT_LEFT, :] = vp_ref[...]
    v_s[ATT_LEFT:ATT_LEFT + tq, :] = vc_ref[...]
    v_s[ATT_LEFT + tq:, :] = jnp.zeros((BAND_PAD - BAND, ATT_WIDTH), BF16)

    gw = ATT_GROUP_WIDTH
    rr = lax.broadcasted_iota(jnp.int32, (gw, gw), 0) // CHUNK
    ll = lax.broadcasted_iota(jnp.int32, (gw, gw), 1) // ATT_HEAD_DIM
    own_head = rr == ll
    lane_head = lax.broadcasted_iota(jnp.int32, (CHUNK, gw), 1) // ATT_HEAD_DIM
    groups = range(ATT_GROUPS)
    n_chunks = tq // CHUNK

    def scores_stage(c, slot):
        q0 = c * CHUNK if isinstance(c, int) else pl.multiple_of(c * CHUNK, CHUNK)
        for g in groups:
            lanes = slice(g * gw, (g + 1) * gw)
            qg = q_ref[pl.ds(q0, CHUNK), lanes]
            q_bd = jnp.where(own_head, jnp.concatenate([qg] * ATT_GROUP_HEADS, axis=0), jnp.zeros((), BF16))
            kb = k_s[pl.ds(q0, BAND), lanes]
            st = lax.dot_general(kb, q_bd, (((1,), (1,)), ((), ())),
                                 preferred_element_type=F32) + bias_ref[g]
            s_s[slot, g] = st
            max_s[slot, g] = jnp.broadcast_to(jnp.max(st, axis=0, keepdims=True), (8, gw))

    def mask_stage(c, slot):
        n_before = jnp.maximum(LEFT_CHUNKS - (i * n_chunks + c), 0)

        def mask_slab(j, carry):
            r0 = pl.multiple_of(j * CHUNK, CHUNK)
            for g in groups:
                s_s[slot, g, pl.ds(r0, CHUNK), :] = jnp.full((CHUNK, gw), -jnp.inf, F32)
            return carry

        lax.fori_loop(0, n_before, mask_slab, 0)

        @pl.when(n_before > 0)
        def _():
            for g in groups:
                max_s[slot, g] = jnp.broadcast_to(jnp.max(s_s[slot, g], axis=0, keepdims=True), (8, gw))

    def softmax_stage(slot):
        for g in groups:
            e = jnp.exp2(s_s[slot, g] - max_s[slot, g, 0:1, :])
            den_s[slot, g] = jnp.broadcast_to(jnp.sum(e, axis=0, keepdims=True), (8, gw))
            eb = jnp.concatenate([e.astype(BF16), jnp.zeros((BAND_PAD - BAND, gw), BF16)], axis=0)
            for r in range(0, BAND_PAD, LANES):
                pt_s[slot, g, :, r:r + LANES] = eb[r:r + LANES, :].T

    def pv_stage(c, slot):
        q0 = c * CHUNK if isinstance(c, int) else pl.multiple_of(c * CHUNK, CHUNK)
        for g in groups:
            lanes = slice(g * gw, (g + 1) * gw)
            vb = v_s[pl.ds(q0, BAND_PAD), lanes]
            o4 = jnp.dot(pt_s[slot, g], vb, preferred_element_type=F32)
            out = o4[0:CHUNK]
            for hj in range(1, ATT_GROUP_HEADS):
                out = jnp.where(lane_head == hj, o4[hj * CHUNK:(hj + 1) * CHUNK], out)
            o_s[slot, :, lanes] = out
            inv_s[slot, g] = 1.0 / den_s[slot, g]

    def finalize_stage(c, slot):
        q0 = c * CHUNK if isinstance(c, int) else pl.multiple_of(c * CHUNK, CHUNK)
        scales = []
        for g in groups:
            inv_den = inv_s[slot, g, 0:1, :]
            scale = None
            for half in range(gw // LANES):
                col = jnp.broadcast_to(inv_den[:, half * LANES:(half + 1) * LANES], (LANES, LANES)).T
                for j in range(LANES // CHUNK):
                    hj = half * (LANES // CHUNK) + j
                    blk = jnp.concatenate([col[j * CHUNK:(j + 1) * CHUNK]] * (gw // LANES), axis=1)
                    scale = blk if scale is None else jnp.where(lane_head == hj, blk, scale)
            scales.append(scale)
        att = o_s[slot] * jnp.concatenate(scales, axis=-1)
        o_ref[pl.ds(q0, CHUNK), :] = (_rms(att) * g_ref[...]).astype(o_ref.dtype)

    def block(masked, c, slot):
        exists = lambda chunk: 0 <= chunk < n_chunks
        if exists(c):
            scores_stage(c, slot)
        if exists(c - 1):
            softmax_stage(1 - slot)
        if exists(c - 2):
            pv_stage(c - 2, slot)
        if exists(c - 3):
            finalize_stage(c - 3, 1 - slot)
        if masked and exists(c):
            mask_stage(c, slot)

    def steady_pair(masked, j, carry):
        c = 3 + 2 * j
        for k in range(2):
            scores_stage(c + k, (1 + k) % 2)
            softmax_stage(k % 2)
            pv_stage(c + k - 2, (1 + k) % 2)
            finalize_stage(c + k - 3, k % 2)
            if masked:
                mask_stage(c + k, (1 + k) % 2)
        return carry

    def pipeline(masked):
        n_pairs = (n_chunks - 4) // 2
        for c in range(3):
            block(masked, c, c % 2)
        lax.fori_loop(0, n_pairs, functools.partial(steady_pair, masked), 0)
        for c in range(3 + 2 * n_pairs, n_chunks + 3):
            block(masked, c, c % 2)

    assert n_chunks >= 4 and n_chunks % 2 == 0 and n_chunks >= LEFT_CHUNKS
    pl.when(i == 0)(functools.partial(pipeline, True))
    pl.when(i > 0)(functools.partial(pipeline, False))


def _attention(q, k, v, bias_t, g):
    b, s, w = q.shape
    tq = ATT_TQ
    assert tq % ATT_LEFT == 0
    cur = lambda bi, i: (bi, i, 0)
    prev = lambda bi, i: (bi, jnp.maximum(i * (tq // ATT_LEFT) - 1, 0), 0)
    blk = (None, tq, w)
    left = (None, ATT_LEFT, w)
    return pl.pallas_call(
        _attn_kernel,
        grid=(b, s // tq),
        in_specs=[pl.BlockSpec(blk, cur),
                  pl.BlockSpec(left, prev), pl.BlockSpec(blk, cur),
                  pl.BlockSpec(left, prev), pl.BlockSpec(blk, cur),
                  _resident(bias_t.shape), _resident(g.shape)],
        out_specs=pl.BlockSpec(blk, cur),
        out_shape=jax.ShapeDtypeStruct((b, s, w), BF16),
        scratch_shapes=[pltpu.VMEM((ATT_LEFT + tq, w), BF16),
                        pltpu.VMEM((ATT_LEFT + tq + BAND_PAD - BAND, w), BF16),
                        pltpu.VMEM((2, ATT_GROUPS, BAND, ATT_GROUP_WIDTH), F32),
                        pltpu.VMEM((2, ATT_GROUPS, ATT_GROUP_WIDTH, BAND_PAD), BF16),
                        pltpu.VMEM((2, ATT_GROUPS, 8, ATT_GROUP_WIDTH), F32),
                        pltpu.VMEM((2, ATT_GROUPS, 8, ATT_GROUP_WIDTH), F32),
                        pltpu.VMEM((2, CHUNK, w), F32),
                        pltpu.VMEM((2, ATT_GROUPS, 8, ATT_GROUP_WIDTH), F32)],
        compiler_params=_params("arbitrary", "arbitrary"),
        name="chunk_attn",
    )(q, k, k, v, v, bias_t, g)


def _bias_table(rel_bias):
    n_rel = rel_bias.shape[1]
    n_f = BAND + REL_FUTURE
    f = jnp.concatenate([jnp.broadcast_to(rel_bias[:, n_rel - 1:], (ATT_HEADS, n_f - n_rel)),
                         rel_bias[:, ::-1]], axis=1).astype(F32) * LOG2E
    skew = jnp.broadcast_to(jnp.pad(f, ((0, 0), (0, 1)))[:, None, :], (ATT_HEADS, CHUNK, n_f + 1))
    skew = skew.reshape(ATT_HEADS, CHUNK * (n_f + 1))[:, :CHUNK * n_f].reshape(ATT_HEADS, CHUNK, n_f)
    bias = skew[:, :, REL_FUTURE:]
    bias = bias.reshape(ATT_GROUPS, ATT_GROUP_HEADS, CHUNK, BAND)
    return jnp.transpose(bias, (0, 3, 1, 2)).reshape(ATT_GROUPS, BAND, ATT_GROUP_WIDTH)


_HALO = 8


def _ssd_kernel(xbc_ref, z_ref, dt_ref, convw_ref, convb_ref, dtb_ref, a_ref, dskip_ref,
                expand_ref, g_ref, o_ref, raw_s, act_s, dt_s, y_s, h_s):
    i = pl.program_id(1)
    tb = SSD_TB
    L = SSD_L

    @pl.when(i == 0)
    def _():
        raw_s[0:_HALO, :] = jnp.zeros((_HALO, XBC_WIDTH), F32)
        h_s[...] = jnp.zeros_like(h_s)

    @pl.when(i > 0)
    def _():
        raw_s[0:_HALO, :] = raw_s[tb:tb + _HALO, :]

    raw_s[_HALO:, :] = xbc_ref[...].astype(F32)
    assert SSD_CONV == 4
    raw = raw_s[...]
    back1 = pltpu.roll(raw, 1, axis=0)
    near = convw_ref[3:4, :] * raw + convw_ref[2:3, :] * back1
    far = convw_ref[1:2, :] * raw + convw_ref[0:1, :] * back1
    conv = convb_ref[...] + near[_HALO:] + pltpu.roll(far, 2, axis=0)[_HALO:]
    act_s[...] = _silu(conv)

    dtr = dt_ref[...] + dtb_ref[...]
    dt_s[...] = jnp.maximum(dtr, 0.0) + jnp.log1p(jnp.exp(-jnp.abs(dtr)))

    a = -jnp.exp(a_ref[...]) * LOG2E
    ri = lax.broadcasted_iota(jnp.int32, (L, L), 0)
    ci = lax.broadcasted_iota(jnp.int32, (L, L), 1)
    causal = ri >= ci
    tril = causal.astype(BF16)
    low_lanes = lax.broadcasted_iota(jnp.int32, (L, LANES), 1) < SSD_HEAD_DIM
    expand = expand_ref[...]
    dskip = dskip_ref[...]
    bc0 = SSD_WIDTH
    cc0 = SSD_WIDTH + SSD_GROUPS * SSD_STATE

    def chunk_body(c, carry):
        r0 = pl.multiple_of(c * L, L)
        dtc = dt_s[pl.ds(r0, L), :]
        adt = dtc * a
        cs3 = jnp.dot(tril, jnp.concatenate(_split_bf16(adt, 3), axis=1), preferred_element_type=F32)
        cs = cs3[:, 0:LANES] + cs3[:, LANES:2 * LANES] + cs3[:, 2 * LANES:]
        cs_t = cs.T
        last = cs[L - 1:L, :]
        stacked = jnp.concatenate([dtc, jnp.exp2(last - cs), jnp.exp2(cs)], axis=0)
        wide = jnp.dot(jnp.concatenate(_split_bf16(stacked, 2), axis=1), expand,
                       preferred_element_type=F32)
        dt_w, decay_w, ecs_w = wide[0:L], wide[L:2 * L], wide[2 * L:3 * L]
        cdec = jnp.broadcast_to(jnp.exp2(last), (8, LANES))
        cdec_w = sum(jnp.dot(part, expand[:LANES], preferred_element_type=F32)
                     for part in _split_bf16(cdec, 3))[0:1]

        xs = act_s[pl.ds(r0, L), 0:SSD_WIDTH]
        xdt = xs * dt_w
        xdt_b = xdt.astype(BF16)
        xdec_b = (xdt * decay_w).astype(BF16)
        ys = []
        for g in range(SSD_GROUPS):
            gl = slice(g * GROUP_WIDTH, (g + 1) * GROUP_WIDTH)
            bm = act_s[pl.ds(r0, L), bc0 + g * SSD_STATE:bc0 + (g + 1) * SSD_STATE].astype(BF16)
            cm = act_s[pl.ds(r0, L), cc0 + g * SSD_STATE:cc0 + (g + 1) * SSD_STATE].astype(BF16)
            cb = lax.dot_general(cm, bm, (((1,), (1,)), ((), ())), preferred_element_type=F32)
            hprev = h_s[g]
            y_off = jnp.dot(cm, hprev.astype(BF16), preferred_element_type=F32) * ecs_w[:, gl]
            states = lax.dot_general(bm, xdec_b[:, gl], (((0,), (0,)), ((), ())),
                                     preferred_element_type=F32)
            h_s[g] = hprev * cdec_w[:, gl] + states
            yd = []
            for pr in range(GROUP_WIDTH // LANES):
                hp = g * (GROUP_WIDTH // LANES) + pr
                xp = xdt_b[:, hp * LANES:(hp + 1) * LANES]
                halves = []
                for hh in (2 * hp, 2 * hp + 1):
                    seg = jnp.exp2(jnp.where(causal, cs[:, hh:hh + 1] - cs_t[hh:hh + 1, :], -jnp.inf))
                    halves.append(jnp.dot((cb * seg).astype(BF16), xp, preferred_element_type=F32))
                yd.append(jnp.where(low_lanes, halves[0], halves[1]))
            ys.append(jnp.concatenate(yd, axis=-1) + y_off)
        y = jnp.concatenate(ys, axis=-1) + xs * dskip
        y_s[pl.ds(r0, L), :] = y
        return carry

    lax.fori_loop(0, tb // L, chunk_body, 0, unroll=True)
    gated = y_s[...] * _silu(z_ref[...].astype(F32))
    o_ref[...] = (_rms(gated) * g_ref[...]).astype(o_ref.dtype)


def _ssd(xbc, z, dt_raw, conv_w, conv_b, dt_bias, a_log, dskip_w, expand, g):
    b, s, _ = xbc.shape
    tb = SSD_TB
    row = lambda bi, i: (bi, i, 0)
    return pl.pallas_call(
        _ssd_kernel,
        grid=(b, s // tb),
        in_specs=[pl.BlockSpec((None, tb, XBC_WIDTH), row),
                  pl.BlockSpec((None, tb, SSD_WIDTH), row),
                  pl.BlockSpec((None, tb, LANES), row),
                  _resident(conv_w.shape), _resident(conv_b.shape), _resident(dt_bias.shape),
                  _resident(a_log.shape), _resident(dskip_w.shape), _resident(expand.shape),
                  _resident(g.shape)],
        out_specs=pl.BlockSpec((None, tb, SSD_WIDTH), row),
        out_shape=jax.ShapeDtypeStruct((b, s, SSD_WIDTH), BF16),
        scratch_shapes=[pltpu.VMEM((_HALO + tb, XBC_WIDTH), F32),
                        pltpu.VMEM((tb, XBC_WIDTH), F32),
                        pltpu.VMEM((tb, LANES), F32),
                        pltpu.VMEM((tb, SSD_WIDTH), F32),
                        pltpu.VMEM((SSD_GROUPS, SSD_STATE, GROUP_WIDTH), F32)],
        compiler_params=_params("arbitrary", "arbitrary"),
        name="ssd",
    )(xbc, z, dt_raw, conv_w, conv_b, dt_bias, a_log, dskip_w, expand, g)


def _outproj_kernel(att_ref, ssd_ref, x_ref, mods_ref, g_ref, w_ref, x1_ref, h_ref):
    for r0 in range(0, x_ref.shape[0], OUT_SUB):
        rows = slice(r0, r0 + OUT_SUB)
        mix = jnp.dot(att_ref[rows, :], w_ref[0:ATT_WIDTH, :], preferred_element_type=F32)
        mix = mix + jnp.dot(ssd_ref[rows, :], w_ref[ATT_WIDTH:, :], preferred_element_type=F32)
        x1 = x_ref[rows, :] + mods_ref[2:3, :] * mix
        x1_ref[rows, :] = x1
        h = _rms(x1) * g_ref[...]
        h_ref[rows, :] = (h * (1.0 + mods_ref[4:5, :]) + mods_ref[3:4, :]).astype(h_ref.dtype)


def _outproj(att, ssd, x, mods, g, w):
    b, s, d = x.shape
    tm = OUT_TM
    row = lambda bi, i: (bi, i, 0)
    return pl.pallas_call(
        _outproj_kernel,
        grid=(b, s // tm),
        in_specs=[pl.BlockSpec((None, tm, ATT_WIDTH), row),
                  pl.BlockSpec((None, tm, SSD_WIDTH), row),
                  pl.BlockSpec((None, tm, d), row),
                  pl.BlockSpec((None, N_MOD, d), lambda bi, i: (bi, 0, 0)),
                  _resident(g.shape), _resident(w.shape)],
        out_specs=[pl.BlockSpec((None, tm, d), row), pl.BlockSpec((None, tm, d), row)],
        out_shape=[jax.ShapeDtypeStruct((b, s, d), F32), jax.ShapeDtypeStruct((b, s, d), BF16)],
        compiler_params=_params("arbitrary", "arbitrary"),
        name="outproj",
    )(att, ssd, x, mods, g, w)


def _ffn_kernel(final_norm, h_ref, x1_ref, mods_ref, g_ref, wg_ref, wu_ref, wd_ref, o_ref):
    k = pl.program_id(2)

    def hidden_block(base_ref):
        h = h_ref[...]
        gate = jnp.dot(h, wg_ref[...], preferred_element_type=F32)
        up = jnp.dot(h, wu_ref[...], preferred_element_type=F32)
        act = (_silu(gate) * up).astype(BF16)
        for n0 in range(0, o_ref.shape[-1], FFN_TN):
            cols = slice(n0, n0 + FFN_TN)
            o_ref[:, cols] = base_ref[:, cols] + mods_ref[5:6, cols] * jnp.dot(
                act, wd_ref[:, cols], preferred_element_type=F32)

    pl.when(k == 0)(functools.partial(hidden_block, x1_ref))
    pl.when(k > 0)(functools.partial(hidden_block, o_ref))

    if final_norm:
        @pl.when(k == pl.num_programs(2) - 1)
        def _():
            o_ref[...] = _rms(o_ref[...]) * g_ref[...]


def _ffn(h, x1, mods, g_final, w_gate, w_up, w_down):
    b, s, d = x1.shape
    tm, th = FFN_TM, FFN_TH
    hid = w_gate.shape[1]
    row = lambda bi, i, k: (bi, i, 0)
    final_norm = g_final is not None
    g = g_final if final_norm else jnp.ones((1, d), F32)
    return pl.pallas_call(
        functools.partial(_ffn_kernel, final_norm),
        grid=(b, s // tm, hid // th),
        in_specs=[pl.BlockSpec((None, tm, d), row),
                  pl.BlockSpec((None, tm, d), row),
                  pl.BlockSpec((None, N_MOD, d), lambda bi, i, k: (bi, 0, 0)),
                  pl.BlockSpec(g.shape, lambda bi, i, k: (0, 0)),
                  pl.BlockSpec((d, th), lambda bi, i, k: (0, k)),
                  pl.BlockSpec((d, th), lambda bi, i, k: (0, k)),
                  pl.BlockSpec((th, d), lambda bi, i, k: (k, 0))],
        out_specs=pl.BlockSpec((None, tm, d), row),
        out_shape=jax.ShapeDtypeStruct((b, s, d), F32),
        compiler_params=_params("arbitrary", "arbitrary", "arbitrary"),
        name="ffn",
    )(h, x1, mods, g, w_gate, w_up, w_down)


def _pad_lanes(v):
    return jnp.pad(v.astype(F32), (0, LANES - v.shape[0])).reshape(1, LANES)


def kernel(x, c, w_ada, b_ada, g_mix, w_in, rel_bias, conv_w, conv_b, dt_bias, a_log, d_skip,
           g_att_out, g_ssd_out, w_out, g_ffn, w_gate, w_up, w_down, g_final):
    b, s, d = x.shape
    depth = w_ada.shape[0]
    n_main = sum(_PROJ_SPLITS)
    head_of_lane = np.arange(SSD_WIDTH) // SSD_HEAD_DIM
    expand = jnp.asarray(np.arange(2 * LANES)[:, None] % LANES == head_of_lane[None, :], BF16)
    c_pad = jnp.pad(c, ((0, 8 - b % 8 if b % 8 else 0), (0, 0)))
    for l in range(depth):
        mods = _adaln(c_pad, w_ada[l], b_ada[l].reshape(1, -1))[:b].reshape(b, N_MOD, d)
        n_in = w_in.shape[-1]
        q_scale = jnp.asarray(np.where(np.arange(n_in) < ATT_WIDTH, ATT_HEAD_DIM ** -0.5 * LOG2E, 1.0), F32)
        w_t = w_in[l].T
        w_proj = (w_t * q_scale[:, None]).astype(BF16)
        w_dt = jnp.pad(w_proj[n_main:], ((0, LANES - (n_in - n_main)), (0, 0)))
        q, k, v, z, xbc, dt_raw, w_o, w_g, w_u, w_d = _inproj(
            x, mods, g_mix[l].reshape(1, d), w_proj, w_dt, (w_out[l], w_gate[l], w_up[l], w_down[l]))
        att = _attention(q, k, v, _bias_table(rel_bias[l]), g_att_out[l].reshape(1, -1))
        ssd = _ssd(xbc, z, dt_raw, conv_w[l], conv_b[l].reshape(1, -1), _pad_lanes(dt_bias[l]),
                   _pad_lanes(a_log[l]), jnp.repeat(d_skip[l].astype(F32), SSD_HEAD_DIM).reshape(1, -1), expand,
                   g_ssd_out[l].reshape(1, -1))
        x1, h2 = _outproj(att, ssd, x, mods, g_ffn[l].reshape(1, d), w_o)
        g_last = g_final.reshape(1, d) if l == depth - 1 else None
        x = _ffn(h2, x1, mods, g_last, w_g, w_u, w_d)
    return x
```

```python
import functools

import jax
import jax.numpy as jnp
import numpy as np
from jax import lax
from jax.experimental import pallas as pl
from jax.experimental.pallas import tpu as pltpu

F32 = jnp.float32
BF16 = jnp.bfloat16

D_MODEL = 2048
CHUNK = 64
LEFT_CHUNKS = 8
BAND = (LEFT_CHUNKS + 1) * CHUNK
ATT_HEADS = 16
ATT_HEAD_DIM = 64
ATT_WIDTH = ATT_HEADS * ATT_HEAD_DIM
REL_CLIP = 256
REL_FUTURE = CHUNK - 1
SSD_HEADS = 16
SSD_HEAD_DIM = 64
SSD_WIDTH = SSD_HEADS * SSD_HEAD_DIM
SSD_GROUPS = 2
SSD_STATE = 128
SSD_CONV = 4
XBC_WIDTH = SSD_WIDTH + 2 * SSD_GROUPS * SSD_STATE
GROUP_WIDTH = SSD_WIDTH // SSD_GROUPS
FFN_HIDDEN = 5632
N_MOD = 6
EPS = 1e-6

LANES = 128
BF16_SUBLANES = 16
MXU_WIDTH = 256
ATT_GROUP_WIDTH = MXU_WIDTH
ATT_GROUP_HEADS = ATT_GROUP_WIDTH // ATT_HEAD_DIM
ATT_GROUPS = ATT_WIDTH // ATT_GROUP_WIDTH
BAND_PAD = -(-BAND // LANES) * LANES
VMEM_LIMIT = 60 * 1024 * 1024

ADA_TN = 1024
PROJ_TM = 512
PROJ_TN = 512
ATT_TQ = 1024
ATT_LEFT = LEFT_CHUNKS * CHUNK
SSD_TB = 1024
SSD_L = 128
OUT_TM = 512
OUT_SUB = 256
FFN_TM = 1024
FFN_TH = 512
FFN_TN = 512


def _params(*sem):
    return pltpu.CompilerParams(dimension_semantics=sem, vmem_limit_bytes=VMEM_LIMIT)


def _resident(shape):
    nd = len(shape)
    return pl.BlockSpec(shape, lambda *_: (0,) * nd, pipeline_mode=pl.Buffered(1))


def _rms(x):
    return x * lax.rsqrt(jnp.mean(x * x, axis=-1, keepdims=True) + EPS)


def _silu(x):
    h = 0.5 * x
    return h + h * jnp.tanh(h)


def _split_bf16(x, terms):
    parts = []
    for _ in range(terms):
        p = x.astype(BF16)
        parts.append(p)
        x = x - p.astype(F32)
    return parts


def _adaln_kernel(c_ref, w_ref, b_ref, o_ref):
    cond = _silu(c_ref[...])
    o_ref[...] = jnp.dot(cond.astype(BF16), w_ref[...].astype(BF16),
                         preferred_element_type=F32) + b_ref[...]


def _adaln(c_pad, w, b):
    rows, d = c_pad.shape
    n = w.shape[1]
    return pl.pallas_call(
        _adaln_kernel,
        grid=(n // ADA_TN,),
        in_specs=[pl.BlockSpec((rows, d), lambda j: (0, 0)),
                  pl.BlockSpec((d, ADA_TN), lambda j: (0, j)),
                  pl.BlockSpec((1, ADA_TN), lambda j: (0, j))],
        out_specs=pl.BlockSpec((rows, ADA_TN), lambda j: (0, j)),
        out_shape=jax.ShapeDtypeStruct((rows, n), F32),
        compiler_params=_params("arbitrary"),
        name="adaln",
    )(c_pad, w, b)


_PROJ_SPLITS = (ATT_WIDTH, ATT_WIDTH, ATT_WIDTH, SSD_WIDTH, XBC_WIDTH)
_NT = (((1,), (1,)), ((), ()))


def _cast_rows(n_rows, n_steps):
    rows = next(r for r in range(BF16_SUBLANES, n_rows + 1, BF16_SUBLANES)
                if n_rows % r == 0 and r * n_steps >= n_rows)
    return rows


def _inproj_kernel(n_cast, x_ref, mods_ref, g_ref, w_ref, wdt_ref, *refs):
    cast_in, refs = refs[:n_cast], refs[n_cast:]
    (q_ref, k_ref, v_ref, z_ref, xbc_ref, dt_ref), refs = refs[:6], refs[6:]
    cast_out, (h_s,) = refs[:n_cast], refs[n_cast:]
    for src, dst in zip(cast_in, cast_out):
        dst[...] = src[...].astype(dst.dtype)
    x = x_ref[...]
    h = _rms(x) * g_ref[...]
    h = h * (1.0 + mods_ref[1:2, :]) + mods_ref[0:1, :]
    h_s[...] = h.astype(BF16)
    col = 0
    for ref, width in zip((q_ref, k_ref, v_ref, z_ref, xbc_ref), _PROJ_SPLITS):
        for j in range(0, width, PROJ_TN):
            ref[:, j:j + PROJ_TN] = lax.dot_general(
                h_s[...], w_ref[col + j:col + j + PROJ_TN, :], _NT,
                preferred_element_type=F32).astype(ref.dtype)
        col += width
    dt_ref[...] = lax.dot_general(h_s[...], wdt_ref[...], _NT, preferred_element_type=F32)


def _inproj(x, mods, g, w, w_dt, to_cast):
    b, s, d = x.shape
    tm = PROJ_TM
    steps_per_seq = s // tm
    row = lambda bi, i: (bi, i, 0)
    out_shape = [jax.ShapeDtypeStruct((b, s, w), BF16) for w in _PROJ_SPLITS]
    out_shape.append(jax.ShapeDtypeStruct((b, s, LANES), F32))
    out_specs = [pl.BlockSpec((None, tm, w), row) for w in _PROJ_SPLITS]
    out_specs.append(pl.BlockSpec((None, tm, LANES), row))
    cast_specs = []
    for a in to_cast:
        rows = _cast_rows(a.shape[0], b * steps_per_seq)
        blk = functools.partial(lambda n_blk, bi, i: (jnp.minimum(bi * steps_per_seq + i, n_blk - 1), 0),
                                a.shape[0] // rows)
        cast_specs.append(pl.BlockSpec((rows, a.shape[1]), blk))
        out_shape.append(jax.ShapeDtypeStruct(a.shape, BF16))
    return pl.pallas_call(
        functools.partial(_inproj_kernel, len(to_cast)),
        grid=(b, steps_per_seq),
        in_specs=[pl.BlockSpec((None, tm, d), row),
                  pl.BlockSpec((None, N_MOD, d), lambda bi, i: (bi, 0, 0)),
                  _resident(g.shape), _resident(w.shape), _resident(w_dt.shape)] + cast_specs,
        out_specs=out_specs + cast_specs,
        out_shape=out_shape,
        scratch_shapes=[pltpu.VMEM((tm, d), BF16)],
        compiler_params=_params("arbitrary", "arbitrary"),
        name="inproj",
    )(x, mods, g, w, w_dt, *to_cast)


LOG2E = 1.4426950408889634


def _attn_kernel(q_ref, kp_ref, kc_ref, vp_ref, vc_ref, bias_ref, g_ref, o_ref,
                 k_s, v_s, s_s, pt_s, max_s, den_s, o_s, inv_s):
    i = pl.program_id(1)
    tq = ATT_TQ
    k_s[0:ATT_LEFT, :] = kp_ref[...]
    k_s[ATT_LEFT:, :] = kc_ref[...]
    v_s[0:ATT_LEFT, :] = vp_ref[...]
    v_s[ATT_LEFT:ATT_LEFT + tq, :] = vc_ref[...]
    v_s[ATT_LEFT + tq:, :] = jnp.zeros((BAND_PAD - BAND, ATT_WIDTH), BF16)

    gw = ATT_GROUP_WIDTH
    rr = lax.broadcasted_iota(jnp.int32, (gw, gw), 0) // CHUNK
    ll = lax.broadcasted_iota(jnp.int32, (gw, gw), 1) // ATT_HEAD_DIM
    own_head = rr == ll
    lane_head = lax.broadcasted_iota(jnp.int32, (CHUNK, gw), 1) // ATT_HEAD_DIM
    groups = range(ATT_GROUPS)
    n_chunks = tq // CHUNK

    def scores_stage(c, slot):
        q0 = c * CHUNK if isinstance(c, int) else pl.multiple_of(c * CHUNK, CHUNK)
        for g in groups:
            lanes = slice(g * gw, (g + 1) * gw)
            qg = q_ref[pl.ds(q0, CHUNK), lanes]
            q_bd = jnp.where(own_head, jnp.concatenate([qg] * ATT_GROUP_HEADS, axis=0), jnp.zeros((), BF16))
            kb = k_s[pl.ds(q0, BAND), lanes]
            st = lax.dot_general(kb, q_bd, (((1,), (1,)), ((), ())),
                                 preferred_element_type=F32) + bias_ref[g]
            s_s[slot, g] = st
            max_s[slot, g] = jnp.broadcast_to(jnp.max(st, axis=0, keepdims=True), (8, gw))

    def mask_stage(c, slot):
        n_before = jnp.maximum(LEFT_CHUNKS - (i * n_chunks + c), 0)

        def mask_slab(j, carry):
            r0 = pl.multiple_of(j * CHUNK, CHUNK)
            for g in groups:
                s_s[slot, g, pl.ds(r0, CHUNK), :] = jnp.full((CHUNK, gw), -jnp.inf, F32)
            return carry

        lax.fori_loop(0, n_before, mask_slab, 0)

        @pl.when(n_before > 0)
        def _():
            for g in groups:
                max_s[slot, g] = jnp.broadcast_to(jnp.max(s_s[slot, g], axis=0, keepdims=True), (8, gw))

    def softmax_stage(slot):
        for g in groups:
            e = jnp.exp2(s_s[slot, g] - max_s[slot, g, 0:1, :])
            den_s[slot, g] = jnp.broadcast_to(jnp.sum(e, axis=0, keepdims=True), (8, gw))
            eb = jnp.concatenate([e.astype(BF16), jnp.zeros((BAND_PAD - BAND, gw), BF16)], axis=0)
            for r in range(0, BAND_PAD, LANES):
                pt_s[slot, g, :, r:r + LANES] = eb[r:r + LANES, :].T

    def pv_stage(c, slot):
        q0 = c * CHUNK if isinstance(c, int) else pl.multiple_of(c * CHUNK, CHUNK)
        for g in groups:
            lanes = slice(g * gw, (g + 1) * gw)
            vb = v_s[pl.ds(q0, BAND_PAD), lanes]
            o4 = jnp.dot(pt_s[slot, g], vb, preferred_element_type=F32)
            out = o4[0:CHUNK]
            for hj in range(1, ATT_GROUP_HEADS):
                out = jnp.where(lane_head == hj, o4[hj * CHUNK:(hj + 1) * CHUNK], out)
            o_s[slot, :, lanes] = out
            inv_s[slot, g] = 1.0 / den_s[slot, g]

    def finalize_stage(c, slot):
        q0 = c * CHUNK if isinstance(c, int) else pl.multiple_of(c * CHUNK, CHUNK)
        scales = []
        for g in groups:
            inv_den = inv_s[slot, g, 0:1, :]
            scale = None
            for half in range(gw // LANES):
                col = jnp.broadcast_to(inv_den[:, half * LANES:(half + 1) * LANES], (LANES, LANES)).T
                for j in range(LANES // CHUNK):
                    hj = half * (LANES // CHUNK) + j
                    blk = jnp.concatenate([col[j * CHUNK:(j + 1) * CHUNK]] * (gw // LANES), axis=1)
                    scale = blk if scale is None else jnp.where(lane_head == hj, blk, scale)
            scales.append(scale)
        att = o_s[slot] * jnp.concatenate(scales, axis=-1)
        o_ref[pl.ds(q0, CHUNK), :] = (_rms(att) * g_ref[...]).astype(o_ref.dtype)

    def block(masked, c, slot):
        exists = lambda chunk: 0 <= chunk < n_chunks
        if exists(c):
            scores_stage(c, slot)
        if exists(c - 1):
            softmax_stage(1 - slot)
        if exists(c - 2):
            pv_stage(c - 2, slot)
        if exists(c - 3):
            finalize_stage(c - 3, 1 - slot)
        if masked and exists(c):
            mask_stage(c, slot)

    def steady_pair(masked, j, carry):
        c = 3 + 2 * j
        for k in range(2):
            scores_stage(c + k, (1 + k) % 2)
            softmax_stage(k % 2)
            pv_stage(c + k - 2, (1 + k) % 2)
            finalize_stage(c + k - 3, k % 2)
            if masked:
                mask_stage(c + k, (1 + k) % 2)
        return carry

    def pipeline(masked):
        n_pairs = (n_chunks - 4) // 2
        for c in range(3):
            block(masked, c, c % 2)
        lax.fori_loop(0, n_pairs, functools.partial(steady_pair, masked), 0)
        for c in range(3 + 2 * n_pairs, n_chunks + 3):
            block(masked, c, c % 2)

    assert n_chunks >= 4 and n_chunks % 2 == 0 and n_chunks >= LEFT_CHUNKS
    pl.when(i == 0)(functools.partial(pipeline, True))
    pl.when(i > 0)(functools.partial(pipeline, False))


def _attention(q, k, v, bias_t, g):
    b, s, w = q.shape
    tq = ATT_TQ
    assert tq % ATT_LEFT == 0
    cur = lambda bi, i: (bi, i, 0)
    prev = lambda bi, i: (bi, jnp.maximum(i * (tq // ATT_LEFT) - 1, 0), 0)
    blk = (None, tq, w)
    left = (None, ATT_LEFT, w)
    return pl.pallas_call(
        _attn_kernel,
        grid=(b, s // tq),
        in_specs=[pl.BlockSpec(blk, cur),
                  pl.BlockSpec(left, prev), pl.BlockSpec(blk, cur),
                  pl.BlockSpec(left, prev), pl.BlockSpec(blk, cur),
                  _resident(bias_t.shape), _resident(g.shape)],
        out_specs=pl.BlockSpec(blk, cur),
        out_shape=jax.ShapeDtypeStruct((b, s, w), BF16),
        scratch_shapes=[pltpu.VMEM((ATT_LEFT + tq, w), BF16),
                        pltpu.VMEM((ATT_LEFT + tq + BAND_PAD - BAND, w), BF16),
                        pltpu.VMEM((2, ATT_GROUPS, BAND, ATT_GROUP_WIDTH), F32),
                        pltpu.VMEM((2, ATT_GROUPS, ATT_GROUP_WIDTH, BAND_PAD), BF16),
                        pltpu.VMEM((2, ATT_GROUPS, 8, ATT_GROUP_WIDTH), F32),
                        pltpu.VMEM((2, ATT_GROUPS, 8, ATT_GROUP_WIDTH), F32),
                        pltpu.VMEM((2, CHUNK, w), F32),
                        pltpu.VMEM((2, ATT_GROUPS, 8, ATT_GROUP_WIDTH), F32)],
        compiler_params=_params("arbitrary", "arbitrary"),
        name="chunk_attn",
    )(q, k, k, v, v, bias_t, g)


def _bias_table(rel_bias):
    n_rel = rel_bias.shape[1]
    n_f = BAND + REL_FUTURE
    f = jnp.concatenate([jnp.broadcast_to(rel_bias[:, n_rel - 1:], (ATT_HEADS, n_f - n_rel)),
                         rel_bias[:, ::-1]], axis=1).astype(F32) * LOG2E
    skew = jnp.broadcast_to(jnp.pad(f, ((0, 0), (0, 1)))[:, None, :], (ATT_HEADS, CHUNK, n_f + 1))
    skew = skew.reshape(ATT_HEADS, CHUNK * (n_f + 1))[:, :CHUNK * n_f].reshape(ATT_HEADS, CHUNK, n_f)
    bias = skew[:, :, REL_FUTURE:]
    bias = bias.reshape(ATT_GROUPS, ATT_GROUP_HEADS, CHUNK, BAND)
    return jnp.transpose(bias, (0, 3, 1, 2)).reshape(ATT_GROUPS, BAND, ATT_GROUP_WIDTH)


_HALO = 8


def _ssd_kernel(xbc_ref, z_ref, dt_ref, convw_ref, convb_ref, dtb_ref, a_ref, dskip_ref,
                expand_ref, g_ref, o_ref, raw_s, act_s, dt_s, y_s, h_s):
    i = pl.program_id(1)
    tb = SSD_TB
    L = SSD_L

    @pl.when(i == 0)
    def _():
        raw_s[0:_HALO, :] = jnp.zeros((_HALO, XBC_WIDTH), F32)
        h_s[...] = jnp.zeros_like(h_s)

    @pl.when(i > 0)
    def _():
        raw_s[0:_HALO, :] = raw_s[tb:tb + _HALO, :]

    raw_s[_HALO:, :] = xbc_ref[...].astype(F32)
    assert SSD_CONV == 4
    raw = raw_s[...]
    back1 = pltpu.roll(raw, 1, axis=0)
    near = convw_ref[3:4, :] * raw + convw_ref[2:3, :] * back1
    far = convw_ref[1:2, :] * raw + convw_ref[0:1, :] * back1
    conv = convb_ref[...] + near[_HALO:] + pltpu.roll(far, 2, axis=0)[_HALO:]
    act_s[...] = _silu(conv)

    dtr = dt_ref[...] + dtb_ref[...]
    dt_s[...] = jnp.maximum(dtr, 0.0) + jnp.log1p(jnp.exp(-jnp.abs(dtr)))

    a = -jnp.exp(a_ref[...]) * LOG2E
    ri = lax.broadcasted_iota(jnp.int32, (L, L), 0)
    ci = lax.broadcasted_iota(jnp.int32, (L, L), 1)
    causal = ri >= ci
    tril = causal.astype(BF16)
    low_lanes = lax.broadcasted_iota(jnp.int32, (L, LANES), 1) < SSD_HEAD_DIM
    expand = expand_ref[...]
    dskip = dskip_ref[...]
    bc0 = SSD_WIDTH
    cc0 = SSD_WIDTH + SSD_GROUPS * SSD_STATE

    def chunk_body(c, carry):
        r0 = pl.multiple_of(c * L, L)
        dtc = dt_s[pl.ds(r0, L), :]
        adt = dtc * a
        cs3 = jnp.dot(tril, jnp.concatenate(_split_bf16(adt, 3), axis=1), preferred_element_type=F32)
        cs = cs3[:, 0:LANES] + cs3[:, LANES:2 * LANES] + cs3[:, 2 * LANES:]
        cs_t = cs.T
        last = cs[L - 1:L, :]
        stacked = jnp.concatenate([dtc, jnp.exp2(last - cs), jnp.exp2(cs)], axis=0)
        wide = jnp.dot(jnp.concatenate(_split_bf16(stacked, 2), axis=1), expand,
                       preferred_element_type=F32)
        dt_w, decay_w, ecs_w = wide[0:L], wide[L:2 * L], wide[2 * L:3 * L]
        cdec = jnp.broadcast_to(jnp.exp2(last), (8, LANES))
        cdec_w = sum(jnp.dot(part, expand[:LANES], preferred_element_type=F32)
                     for part in _split_bf16(cdec, 3))[0:1]

        xs = act_s[pl.ds(r0, L), 0:SSD_WIDTH]
        xdt = xs * dt_w
        xdt_b = xdt.astype(BF16)
        xdec_b = (xdt * decay_w).astype(BF16)
        ys = []
        for g in range(SSD_GROUPS):
            gl = slice(g * GROUP_WIDTH, (g + 1) * GROUP_WIDTH)
            bm = act_s[pl.ds(r0, L), bc0 + g * SSD_STATE:bc0 + (g + 1) * SSD_STATE].astype(BF16)
            cm = act_s[pl.ds(r0, L), cc0 + g * SSD_STATE:cc0 + (g + 1) * SSD_STATE].astype(BF16)
            cb = lax.dot_general(cm, bm, (((1,), (1,)), ((), ())), preferred_element_type=F32)
            hprev = h_s[g]
            y_off = jnp.dot(cm, hprev.astype(BF16), preferred_element_type=F32) * ecs_w[:, gl]
            states = lax.dot_general(bm, xdec_b[:, gl], (((0,), (0,)), ((), ())),
                                     preferred_element_type=F32)
            h_s[g] = hprev * cdec_w[:, gl] + states
            yd = []
            for pr in range(GROUP_WIDTH // LANES):
                hp = g * (GROUP_WIDTH // LANES) + pr
                xp = xdt_b[:, hp * LANES:(hp + 1) * LANES]
                halves = []
                for hh in (2 * hp, 2 * hp + 1):
                    seg = jnp.exp2(jnp.where(causal, cs[:, hh:hh + 1] - cs_t[hh:hh + 1, :], -jnp.inf))
                    halves.append(jnp.dot((cb * seg).astype(BF16), xp, preferred_element_type=F32))
                yd.append(jnp.where(low_lanes, halves[0], halves[1]))
            ys.append(jnp.concatenate(yd, axis=-1) + y_off)
        y = jnp.concatenate(ys, axis=-1) + xs * dskip
        y_s[pl.ds(r0, L), :] = y
        return carry

    lax.fori_loop(0, tb // L, chunk_body, 0, unroll=True)
    gated = y_s[...] * _silu(z_ref[...].astype(F32))
    o_ref[...] = (_rms(gated) * g_ref[...]).astype(o_ref.dtype)


def _ssd(xbc, z, dt_raw, conv_w, conv_b, dt_bias, a_log, dskip_w, expand, g):
    b, s, _ = xbc.shape
    tb = SSD_TB
    row = lambda bi, i: (bi, i, 0)
    return pl.pallas_call(
        _ssd_kernel,
        grid=(b, s // tb),
        in_specs=[pl.BlockSpec((None, tb, XBC_WIDTH), row),
                  pl.BlockSpec((None, tb, SSD_WIDTH), row),
                  pl.BlockSpec((None, tb, LANES), row),
                  _resident(conv_w.shape), _resident(conv_b.shape), _resident(dt_bias.shape),
                  _resident(a_log.shape), _resident(dskip_w.shape), _resident(expand.shape),
                  _resident(g.shape)],
        out_specs=pl.BlockSpec((None, tb, SSD_WIDTH), row),
        out_shape=jax.ShapeDtypeStruct((b, s, SSD_WIDTH), BF16),
        scratch_shapes=[pltpu.VMEM((_HALO + tb, XBC_WIDTH), F32),
                        pltpu.VMEM((tb, XBC_WIDTH), F32),
                        pltpu.VMEM((tb, LANES), F32),
                        pltpu.VMEM((tb, SSD_WIDTH), F32),
                        pltpu.VMEM((SSD_GROUPS, SSD_STATE, GROUP_WIDTH), F32)],
        compiler_params=_params("arbitrary", "arbitrary"),
        name="ssd",
    )(xbc, z, dt_raw, conv_w, conv_b, dt_bias, a_log, dskip_w, expand, g)


def _outproj_kernel(att_ref, ssd_ref, x_ref, mods_ref, g_ref, w_ref, x1_ref, h_ref):
    for r0 in range(0, x_ref.shape[0], OUT_SUB):
        rows = slice(r0, r0 + OUT_SUB)
        mix = jnp.dot(att_ref[rows, :], w_ref[0:ATT_WIDTH, :], preferred_element_type=F32)
        mix = mix + jnp.dot(ssd_ref[rows, :], w_ref[ATT_WIDTH:, :], preferred_element_type=F32)
        x1 = x_ref[rows, :] + mods_ref[2:3, :] * mix
        x1_ref[rows, :] = x1
        h = _rms(x1) * g_ref[...]
        h_ref[rows, :] = (h * (1.0 + mods_ref[4:5, :]) + mods_ref[3:4, :]).astype(h_ref.dtype)


def _outproj(att, ssd, x, mods, g, w):
    b, s, d = x.shape
    tm = OUT_TM
    row = lambda bi, i: (bi, i, 0)
    return pl.pallas_call(
        _outproj_kernel,
        grid=(b, s // tm),
        in_specs=[pl.BlockSpec((None, tm, ATT_WIDTH), row),
                  pl.BlockSpec((None, tm, SSD_WIDTH), row),
                  pl.BlockSpec((None, tm, d), row),
                  pl.BlockSpec((None, N_MOD, d), lambda bi, i: (bi, 0, 0)),
                  _resident(g.shape), _resident(w.shape)],
        out_specs=[pl.BlockSpec((None, tm, d), row), pl.BlockSpec((None, tm, d), row)],
        out_shape=[jax.ShapeDtypeStruct((b, s, d), F32), jax.ShapeDtypeStruct((b, s, d), BF16)],
        compiler_params=_params("arbitrary", "arbitrary"),
        name="outproj",
    )(att, ssd, x, mods, g, w)


def _ffn_kernel(final_norm, h_ref, x1_ref, mods_ref, g_ref, wg_ref, wu_ref, wd_ref, o_ref):
    k = pl.program_id(2)
    last = pl.num_programs(2) - 1

    def hidden_block(base_ref, norm):
        h = h_ref[...]
        gate = jnp.dot(h, wg_ref[...], preferred_element_type=F32)
        up = jnp.dot(h, wu_ref[...], preferred_element_type=F32)
        act = (_silu(gate) * up).astype(BF16)
        for n0 in range(0, o_ref.shape[-1], FFN_TN):
            cols = slice(n0, n0 + FFN_TN)
            o_ref[:, cols] = base_ref[:, cols] + mods_ref[5:6, cols] * jnp.dot(
                act, wd_ref[:, cols], preferred_element_type=F32)
        if norm:
            o_ref[...] = _rms(o_ref[...]) * g_ref[...]

    first_step = functools.partial(hidden_block, x1_ref, False)
    middle_step = functools.partial(hidden_block, o_ref, False)
    last_step = functools.partial(hidden_block, o_ref, final_norm)
    lax.cond((k > 0) & (k < last), middle_step, lambda: lax.cond(k == 0, first_step, last_step))


def _ffn(h, x1, mods, g_final, w_gate, w_up, w_down):
    b, s, d = x1.shape
    tm, th = FFN_TM, FFN_TH
    hid = w_gate.shape[1]
    row = lambda bi, i, k: (bi, i, 0)
    final_norm = g_final is not None
    g = g_final if final_norm else jnp.ones((1, d), F32)
    assert hid // th >= 2, "the first and the last hidden block are distinct grid steps"
    return pl.pallas_call(
        functools.partial(_ffn_kernel, final_norm),
        grid=(b, s // tm, hid // th),
        in_specs=[pl.BlockSpec((None, tm, d), row),
                  pl.BlockSpec((None, tm, d), row),
                  pl.BlockSpec((None, N_MOD, d), lambda bi, i, k: (bi, 0, 0)),
                  pl.BlockSpec(g.shape, lambda bi, i, k: (0, 0)),
                  pl.BlockSpec((d, th), lambda bi, i, k: (0, k)),
                  pl.BlockSpec((d, th), lambda bi, i, k: (0, k)),
                  pl.BlockSpec((th, d), lambda bi, i, k: (k, 0))],
        out_specs=pl.BlockSpec((None, tm, d), row),
        out_shape=jax.ShapeDtypeStruct((b, s, d), F32),
        compiler_params=_params("arbitrary", "arbitrary", "arbitrary"),
        name="ffn",
    )(h, x1, mods, g, w_gate, w_up, w_down)


def _pad_lanes(v):
    return jnp.pad(v.astype(F32), (0, LANES - v.shape[0])).reshape(1, LANES)


def kernel(x, c, w_ada, b_ada, g_mix, w_in, rel_bias, conv_w, conv_b, dt_bias, a_log, d_skip,
           g_att_out, g_ssd_out, w_out, g_ffn, w_gate, w_up, w_down, g_final):
    b, s, d = x.shape
    depth = w_ada.shape[0]
    n_main = sum(_PROJ_SPLITS)
    head_of_lane = np.arange(SSD_WIDTH) // SSD_HEAD_DIM
    expand = jnp.asarray(np.arange(2 * LANES)[:, None] % LANES == head_of_lane[None, :], BF16)
    c_pad = jnp.pad(c, ((0, 8 - b % 8 if b % 8 else 0), (0, 0)))
    for l in range(depth):
        mods = _adaln(c_pad, w_ada[l], b_ada[l].reshape(1, -1))[:b].reshape(b, N_MOD, d)
        n_in = w_in.shape[-1]
        q_scale = jnp.asarray(np.where(np.arange(n_in) < ATT_WIDTH, ATT_HEAD_DIM ** -0.5 * LOG2E, 1.0), F32)
        w_t = w_in[l].T
        w_proj = (w_t * q_scale[:, None]).astype(BF16)
        w_dt = jnp.pad(w_proj[n_main:], ((0, LANES - (n_in - n_main)), (0, 0)))
        q, k, v, z, xbc, dt_raw, w_o, w_g, w_u, w_d = _inproj(
            x, mods, g_mix[l].reshape(1, d), w_proj, w_dt, (w_out[l], w_gate[l], w_up[l], w_down[l]))
        att = _attention(q, k, v, _bias_table(rel_bias[l]), g_att_out[l].reshape(1, -1))
        ssd = _ssd(xbc, z, dt_raw, conv_w[l], conv_b[l].reshape(1, -1), _pad_lanes(dt_bias[l]),
                   _pad_lanes(a_log[l]), jnp.repeat(d_skip[l].astype(F32), SSD_HEAD_DIM).reshape(1, -1), expand,
                   g_ssd_out[l].reshape(1, -1))
        x1, h2 = _outproj(att, ssd, x, mods, g_ffn[l].reshape(1, d), w_o)
        g_last = g_final.reshape(1, d) if l == depth - 1 else None
        x = _ffn(h2, x1, mods, g_last, w_g, w_u, w_d)
    return x
```
